```python
import math
import jax
import jax.numpy as jnp
from jax import lax
import numpy as np

D_MODEL = 2048
BATCH = 1
SEQ = 8192
DEPTH = 2

GRID_W = 64
CTX_LEN = 256
HEAD_DIM = 128
RET_W = 3 * D_MODEL // 8
RET_H = RET_W // HEAD_DIM
HY_W = D_MODEL // 4
HG_W = D_MODEL - RET_W - HY_W
HG_H = HG_W // HEAD_DIM
MIX_W = RET_W + HY_W + HG_W
RET_COLS = 4 * RET_W
HY_COLS = 3 * HY_W
HG_COLS = 5 * HG_W
IN_W = RET_COLS + HY_COLS + HG_COLS
CHUNK = 64
ROPE_BASE = 10000.0
EPS = 1e-6
MASK_LOG = -1e30
N_MOD = 6
HY_BANDS = 16
HY_EMB = 2 * HY_BANDS + 1
HY_FILT_W = 64
HY_MAX_DECAY = math.log(1e-2) / 0.3
HY_MIN_DECAY = math.log(1e-2) / 1.5
N_EXPERTS = 32
TOP_K = 4
D_EXPERT = D_MODEL
SWIGLU_LIMIT = 7.0
SWIGLU_ALPHA = 1.702
MOE_BLOCK = 256

kernel_name = 'hybrid_retention_hyena_hgrn2_moe_dit'


def rmsnorm(x, g):
    xf = x.astype(jnp.float32)
    y = xf * lax.rsqrt(jnp.mean(xf * xf, axis=-1, keepdims=True) + EPS)
    return (y * g).astype(x.dtype)


def modulate(h, shift, scale):
    return h * (1.0 + scale) + shift


def heads(t, n_heads):
    b, l, w = t.shape
    return t.reshape(b, l, n_heads, w // n_heads).transpose(0, 2, 1, 3)


def head_norm(o):
    o = o * lax.rsqrt(jnp.mean(o * o, axis=-1, keepdims=True) + EPS)
    b, h, l, d = o.shape
    return o.transpose(0, 2, 1, 3).reshape(b, l, h * d)


def chunk_gla(q, k, v, log_f):
    b, h, l, dk = q.shape
    dv = v.shape[-1]
    n = l // CHUNK

    def blocks(t):
        return jnp.moveaxis(t.astype(jnp.float32).reshape(b, h, n, CHUNK, t.shape[-1]), 2, 0)

    causal = jnp.tril(jnp.ones((CHUNK, CHUNK), dtype=bool))[:, :, None]

    def step(state, blk):
        qb, kb, vb, gb = blk
        cum = jnp.cumsum(gb, axis=2)
        rel = jnp.where(causal, cum[:, :, :, None, :] - cum[:, :, None, :, :], MASK_LOG)
        scores = jnp.einsum('bhtk,bhsk,bhtsk->bhts', qb, kb, jnp.exp(rel))
        out = (jnp.einsum('bhts,bhsv->bhtv', scores, vb)
               + jnp.einsum('bhtk,bhkv->bhtv', qb * jnp.exp(cum), state))
        last = cum[:, :, -1:, :]
        state = (jnp.exp(last[:, :, 0, :])[..., None] * state
                 + jnp.einsum('bhsk,bhsv->bhkv', kb * jnp.exp(last - cum), vb))
        return state, out

    s0 = jnp.zeros((b, h, dk, dv), jnp.float32)
    _, o = lax.scan(step, s0, (blocks(q), blocks(k), blocks(v), blocks(log_f)))
    return jnp.moveaxis(o, 0, 2).reshape(b, h, l, dv)


def prefix_bidir_gla(q, k_f, k_b, v, lf_f, lf_b, ctx_len):
    def rev(t):
        return jnp.concatenate([jnp.flip(t[:, :, :ctx_len], axis=2), jnp.flip(t[:, :, ctx_len:], axis=2)], axis=2)
    fwd = chunk_gla(q, k_f, v, lf_f)
    bwd = rev(chunk_gla(rev(q), rev(k_b), rev(v), rev(lf_b)))
    return fwd + bwd


def rope_tables(ctx_len, n_rows):
    nf = HEAD_DIM // 4
    inv = ROPE_BASE ** (-jnp.arange(nf, dtype=jnp.float32) / nf)
    rows = jnp.repeat(jnp.arange(n_rows, dtype=jnp.float32), GRID_W)
    cols = (jnp.arange(n_rows * GRID_W) % GRID_W).astype(jnp.float32)
    ang = jnp.concatenate([rows[:, None] * inv, cols[:, None] * inv], axis=-1)
    ang = jnp.concatenate([jnp.zeros((ctx_len, HEAD_DIM // 2), jnp.float32), ang], axis=0)
    return jnp.cos(ang), jnp.sin(ang)


def apply_rope(x, cos_t, sin_t):
    half = x.shape[-1] // 2
    x1, x2 = x[..., :half], x[..., half:]
    return jnp.concatenate([x1 * cos_t - x2 * sin_t, x1 * sin_t + x2 * cos_t], axis=-1)


def retention(z, decay_logit, cos_t, sin_t, ctx_len):
    q, k, v, g = jnp.split(z, 4, axis=-1)
    q = apply_rope(heads(q, RET_H), cos_t, sin_t)
    k = apply_rope(heads(k, RET_H), cos_t, sin_t) * HEAD_DIM ** -0.5
    v = heads(v, RET_H)
    log_gamma = jax.nn.log_sigmoid(decay_logit.astype(jnp.float32))
    lg_f = jnp.broadcast_to(log_gamma[0][None, :, None, None], q.shape)
    lg_b = jnp.broadcast_to(log_gamma[1][None, :, None, None], q.shape)
    o = prefix_bidir_gla(q, k, k, v, lg_f, lg_b, ctx_len)
    return jax.nn.silu(g) * head_norm(o)


def hgrn2(z, lb_logits, layer, ctx_len):
    q, fz_f, fz_b, inp, g = jnp.split(z, 5, axis=-1)
    p = jax.nn.softmax(lb_logits.astype(jnp.float32), axis=0)
    lb = jnp.cumsum(p, axis=0)[layer] - p[0]

    def forget(fz, lbd):
        s = jax.nn.sigmoid(fz.astype(jnp.float32))
        f = lbd + (1.0 - lbd) * s
        key = (1.0 - lbd) * (1.0 - s)
        return heads(jnp.log(f), HG_H), heads(key, HG_H)

    lf_f, k_f = forget(fz_f, lb[0])
    lf_b, k_b = forget(fz_b, lb[1])
    o = prefix_bidir_gla(heads(jax.nn.silu(q), HG_H), k_f, k_b, heads(inp, HG_H), lf_f, lf_b, ctx_len)
    return jax.nn.sigmoid(g) * head_norm(o)


def dwconv3(u, w, b):
    ch = u.shape[-1]
    y = lax.conv_general_dilated(u, w[:, None, :].astype(u.dtype), window_strides=(1,), padding=((1, 1),),
                                 dimension_numbers=('NWC', 'WIO', 'NWC'), feature_group_count=ch)
    return y + b


def hyena_filters(L, w1, b1, freq, w2, b2, w3):
    t = jnp.linspace(0.0, 1.0, L, dtype=jnp.float32)[:, None]
    w = 2.0 * math.pi * jnp.arange(L, dtype=jnp.float32)[:, None] / L
    f = jnp.linspace(1e-4, HY_BANDS - 1, HY_BANDS, dtype=jnp.float32)[None, :]
    feats = jnp.concatenate([t, jnp.cos(f * w), -jnp.sin(f * w)], axis=-1)
    h = jnp.sin(freq * (feats @ w1 + b1))
    h = jnp.sin(freq * (h @ w2 + b2))
    h = (h @ w3).astype(jnp.float32)
    deltas = jnp.abs(jnp.linspace(HY_MIN_DECAY, HY_MAX_DECAY, HY_W, dtype=jnp.float32))
    window = jnp.exp(-t * deltas[None, :])
    return h.reshape(L, 2, 2, HY_W) * window[:, None, None, :]


def long_conv_bidir(u, h_fwd, h_bwd, skip):
    L, ch = h_fwd.shape
    filt = jnp.concatenate([h_fwd, jnp.zeros((1, ch), h_fwd.dtype), jnp.flip(h_bwd[1:], axis=0)], axis=0)
    spec = jnp.fft.rfft(u, n=2 * L, axis=1) * jnp.fft.rfft(filt, n=2 * L, axis=0)[None]
    return jnp.fft.irfft(spec, n=2 * L, axis=1)[:, :L] + skip * u


def hyena_seq(u, conv_w, conv_b, w1, b1, freq, w2, b2, w3, skip):
    L = u.shape[1]
    v, x1, x2 = jnp.split(dwconv3(u, conv_w, conv_b).astype(jnp.float32), 3, axis=-1)
    h = hyena_filters(L, w1, b1, freq, w2, b2, w3)
    y = x1 * long_conv_bidir(v, h[:, 0, 0], h[:, 0, 1], skip[0])
    return x2 * long_conv_bidir(y, h[:, 1, 0], h[:, 1, 1], skip[1])


def moe(h, router_w, router_b, w_gate, b_gate, w_up, b_up, w_down, b_down):
    n_tok, d = h.shape
    logits = (h @ router_w + router_b).astype(jnp.float32)
    top_val, top_idx = lax.top_k(logits, TOP_K)
    top_w = jax.nn.softmax(top_val, axis=-1)
    n_assign = n_tok * TOP_K
    flat_e = top_idx.reshape(n_assign)
    order = jnp.argsort(flat_e)
    sorted_e = flat_e[order]
    sorted_tok = order // TOP_K
    sorted_w = top_w.reshape(n_assign)[order]
    counts = jnp.bincount(flat_e, length=N_EXPERTS)
    padded = (counts + MOE_BLOCK - 1) // MOE_BLOCK * MOE_BLOCK
    pad_end = jnp.cumsum(padded)
    pad_start = pad_end - padded
    grp_start = jnp.cumsum(counts) - counts
    dest = pad_start[sorted_e] + jnp.arange(n_assign) - grp_start[sorted_e]
    n_blocks = -(-(n_assign + N_EXPERTS * (MOE_BLOCK - 1)) // MOE_BLOCK)
    n_slots = n_blocks * MOE_BLOCK
    slot_tok = jnp.zeros((n_slots,), jnp.int32).at[dest].set(sorted_tok)
    slot_w = jnp.zeros((n_slots,), jnp.float32).at[dest].set(sorted_w)
    blk_e = jnp.minimum(jnp.searchsorted(pad_end, jnp.arange(n_blocks) * MOE_BLOCK, side='right'), N_EXPERTS - 1)
    xb = h[slot_tok].reshape(n_blocks, MOE_BLOCK, d)

    def expert_block(args):
        xe, e = args
        gate = jnp.minimum(xe @ w_gate[e] + b_gate[e], SWIGLU_LIMIT)
        up = jnp.clip(xe @ w_up[e] + b_up[e], -SWIGLU_LIMIT, SWIGLU_LIMIT)
        act = gate * jax.nn.sigmoid(SWIGLU_ALPHA * gate) * (up + 1.0)
        return act @ w_down[e] + b_down[e]

    y = lax.map(expert_block, (xb, blk_e)).reshape(n_slots, d)
    contrib = (y * slot_w[:, None]).astype(h.dtype)
    return jnp.zeros_like(h).at[slot_tok].add(contrib)


def setup_inputs(seed: int = 0) -> dict:
    key = jax.random.key(seed)
    ks = jax.random.split(key, 30)

    def nrm(i, shape, scale):
        return scale * jax.random.normal(ks[i], shape, jnp.float32)

    gamma0 = 1.0 - 2.0 ** (-5.0 - jnp.arange(RET_H, dtype=jnp.float32))
    logit0 = jnp.log(gamma0) - jnp.log1p(-gamma0)
    return {
        'x': nrm(0, (BATCH, SEQ, D_MODEL), 1.0),
        'c': nrm(1, (BATCH, D_MODEL), 1.0),
        'ctx': nrm(2, (BATCH, CTX_LEN, D_MODEL), 1.0),
        'c_ctx': nrm(3, (D_MODEL,), 1.0),
        'norm_mix_g': 1.0 + nrm(4, (DEPTH, D_MODEL), 0.02),
        'norm_ffn_g': 1.0 + nrm(5, (DEPTH, D_MODEL), 0.02),
        'final_norm_g': 1.0 + nrm(6, (D_MODEL,), 0.02),
        'ada_w': nrm(7, (DEPTH, D_MODEL, N_MOD * D_MODEL), 0.5 * D_MODEL ** -0.5),
        'ada_b': nrm(8, (DEPTH, N_MOD * D_MODEL), 0.02),
        'w_in': nrm(9, (DEPTH, D_MODEL, IN_W), D_MODEL ** -0.5),
        'w_out': nrm(10, (DEPTH, MIX_W, D_MODEL), MIX_W ** -0.5),
        'ret_decay_logit': logit0 + nrm(11, (DEPTH, 2, RET_H), 0.05),
        'hy_conv_w': nrm(12, (DEPTH, 3, HY_COLS), 3 ** -0.5),
        'hy_conv_b': nrm(13, (DEPTH, HY_COLS), 0.02),
        'hy_filt_w1': nrm(14, (DEPTH, HY_EMB, HY_FILT_W), HY_EMB ** -0.5),
        'hy_filt_b1': nrm(15, (DEPTH, HY_FILT_W), 0.02),
        'hy_filt_freq': 1.0 + nrm(16, (DEPTH, HY_FILT_W), 0.02),
        'hy_filt_w2': nrm(17, (DEPTH, HY_FILT_W, HY_FILT_W), HY_FILT_W ** -0.5),
        'hy_filt_b2': nrm(18, (DEPTH, HY_FILT_W), 0.02),
        'hy_filt_w3': nrm(19, (DEPTH, HY_FILT_W, 4 * HY_W), 0.05 * HY_FILT_W ** -0.5),
        'hy_skip': nrm(20, (DEPTH, 2, HY_W), 0.1),
        'hg_lb_logits': nrm(21, (DEPTH, 2, HG_W), 0.1),
        'router_w': nrm(22, (DEPTH, D_MODEL, N_EXPERTS), D_MODEL ** -0.5),
        'router_b': nrm(23, (DEPTH, N_EXPERTS), 0.01),
        'moe_w_gate': nrm(24, (DEPTH, N_EXPERTS, D_MODEL, D_EXPERT), D_MODEL ** -0.5),
        'moe_b_gate': nrm(25, (DEPTH, N_EXPERTS, D_EXPERT), 0.01),
        'moe_w_up': nrm(26, (DEPTH, N_EXPERTS, D_MODEL, D_EXPERT), D_MODEL ** -0.5),
        'moe_b_up': nrm(27, (DEPTH, N_EXPERTS, D_EXPERT), 0.01),
        'moe_w_down': nrm(28, (DEPTH, N_EXPERTS, D_EXPERT, D_MODEL), D_EXPERT ** -0.5),
        'moe_b_down': nrm(29, (DEPTH, N_EXPERTS, D_MODEL), 0.01),
    }


def reference(x, c, ctx, c_ctx, norm_mix_g, norm_ffn_g, final_norm_g, ada_w, ada_b, w_in, w_out,
              ret_decay_logit, hy_conv_w, hy_conv_b, hy_filt_w1, hy_filt_b1, hy_filt_freq, hy_filt_w2,
              hy_filt_b2, hy_filt_w3, hy_skip, hg_lb_logits, router_w, router_b, moe_w_gate, moe_b_gate,
              moe_w_up, moe_b_up, moe_w_down, moe_b_down):
    bsz, seq, d = x.shape
    ctx_len = ctx.shape[1]
    n_rows = seq // GRID_W
    cos_t, sin_t = rope_tables(ctx_len, n_rows)
    cond = jax.nn.silu(c.astype(jnp.float32))
    cond_ctx = jax.nn.silu(c_ctx.astype(jnp.float32))
    xl, xc = x, ctx
    for l in range(DEPTH):
        last = l == DEPTH - 1
        mod = cond @ ada_w[l] + ada_b[l]
        mod_ctx = cond_ctx @ ada_w[l] + ada_b[l]
        sh1, sc1, g1, sh2, sc2, g2 = jnp.split(mod[:, None, :], N_MOD, axis=-1)
        csh1, csc1, cg1, csh2, csc2, cg2 = jnp.split(mod_ctx, N_MOD, axis=-1)

        h = jnp.concatenate([modulate(rmsnorm(xc, norm_mix_g[l]), csh1, csc1),
                             modulate(rmsnorm(xl, norm_mix_g[l]), sh1, sc1)], axis=1)
        z = h @ w_in[l]
        z_ret, z_hy, z_hg = jnp.split(z, [RET_COLS, RET_COLS + HY_COLS], axis=-1)
        o_ret = retention(z_ret, ret_decay_logit[l], cos_t, sin_t, ctx_len)
        o_hg = hgrn2(z_hg, hg_lb_logits, l, ctx_len)
        hy_p = (hy_conv_w[l], hy_conv_b[l], hy_filt_w1[l], hy_filt_b1[l], hy_filt_freq[l],
                hy_filt_w2[l], hy_filt_b2[l], hy_filt_w3[l], hy_skip[l])
        moe_p = (router_w[l], router_b[l], moe_w_gate[l], moe_b_gate[l], moe_w_up[l], moe_b_up[l],
                 moe_w_down[l], moe_b_down[l])
        o_hy_l = hyena_seq(z_hy[:, ctx_len:], *hy_p)
        y_l = jnp.concatenate([o_ret[:, ctx_len:], o_hy_l, o_hg[:, ctx_len:]], axis=-1) @ w_out[l]
        xl = xl + g1 * y_l
        hl2 = modulate(rmsnorm(xl, norm_ffn_g[l]), sh2, sc2)

        if last:
            xl = xl + g2 * moe(hl2.reshape(bsz * seq, d), *moe_p).reshape(bsz, seq, d)
        else:
            o_hy_c = hyena_seq(z_hy[:, :ctx_len], *hy_p)
            y_c = jnp.concatenate([o_ret[:, :ctx_len], o_hy_c, o_hg[:, :ctx_len]], axis=-1) @ w_out[l]
            xc = xc + cg1 * y_c
            hc2 = modulate(rmsnorm(xc, norm_ffn_g[l]), csh2, csc2)
            f = moe(jnp.concatenate([hc2.reshape(bsz * ctx_len, d), hl2.reshape(bsz * seq, d)], axis=0), *moe_p)
            xc = xc + cg2 * f[:bsz * ctx_len].reshape(bsz, ctx_len, d)
            xl = xl + g2 * f[bsz * ctx_len:].reshape(bsz, seq, d)
    return rmsnorm(xl, final_norm_g)
```

```python
import functools
import math

import jax
import jax.numpy as jnp
import numpy as np
from jax import lax
from jax.experimental import pallas as pl
from jax.experimental.pallas import tpu as pltpu

F32 = jnp.float32
BF16 = jnp.bfloat16

D_MODEL = 2048
DEPTH = 2
GRID_W = 64
HEAD_DIM = 128
RET_W = 3 * D_MODEL // 8
RET_H = RET_W // HEAD_DIM
HY_W = D_MODEL // 4
HG_W = D_MODEL - RET_W - HY_W
HG_H = HG_W // HEAD_DIM
RET_COLS = 4 * RET_W
HY_COLS = 3 * HY_W
HG_COLS = 5 * HG_W
IN_W = RET_COLS + HY_COLS + HG_COLS
ROPE_BASE = 10000.0
EPS = 1e-6
N_MOD = 6
HY_BANDS = 16
HY_MAX_DECAY = math.log(1e-2) / 0.3
HY_MIN_DECAY = math.log(1e-2) / 1.5
N_EXPERTS = 32
TOP_K = 4
SWIGLU_LIMIT = 7.0
SWIGLU_ALPHA = 1.702

LANES = 128
VMEM_LIMIT = 56 * 1024 * 1024

RET_CHUNK = 256
HG_CHUNK = 128
HG_SUB = 16
MOE_TM = 512
MOE_TF = 512


def _cparams(sem):
    return pltpu.CompilerParams(dimension_semantics=sem, vmem_limit_bytes=VMEM_LIMIT)


def _dot(a, b):
    return jnp.dot(a, b, preferred_element_type=F32)


def _dot_nt(a, b):
    return lax.dot_general(a, b, (((1,), (1,)), ((), ())), preferred_element_type=F32)


def _adaln_kernel(c_ref, w_ref, b_ref, o_ref):
    c = c_ref[...]
    cond = c * jax.nn.sigmoid(c)
    o_ref[0] = _dot(cond.astype(BF16), w_ref[0].astype(BF16)) + b_ref[0]


def adaln(cvec, ada_w, ada_b, tn=1536):
    depth, d, n = ada_w.shape
    return pl.pallas_call(
        _adaln_kernel,
        grid=(depth, n // tn),
        in_specs=[pl.BlockSpec((8, d), lambda l, j: (0, 0)),
                  pl.BlockSpec((1, d, tn), lambda l, j: (l, 0, j)),
                  pl.BlockSpec((1, 1, tn), lambda l, j: (l, 0, j))],
        out_specs=pl.BlockSpec((1, 8, tn), lambda l, j: (l, 0, j)),
        out_shape=jax.ShapeDtypeStruct((depth, 8, n), F32),
        compiler_params=_cparams(("arbitrary", "arbitrary")),
        name="adaln",
    )(cvec, ada_w, ada_b.reshape(depth, 1, n))


def _norm_mod(x, gain, mod_ref, row0, ctx_len, shift_row):
    y = x * lax.rsqrt(jnp.mean(x * x, axis=-1, keepdims=True) + EPS) * gain
    row = row0 + lax.broadcasted_iota(jnp.int32, (x.shape[0], 1), 0)
    is_ctx = row < ctx_len
    shift = jnp.where(is_ctx, mod_ref[shift_row:shift_row + 1, :], mod_ref[shift_row + 2:shift_row + 3, :])
    scale = jnp.where(is_ctx, mod_ref[shift_row + 1:shift_row + 2, :], mod_ref[shift_row + 3:shift_row + 4, :])
    return y * (1.0 + scale) + shift


def _win_kernel(x_ref, g_ref, mod_ref, w_ref, o_ref, h_scr, *, tm, ctx_len):
    i = pl.program_id(0)

    @pl.when(pl.program_id(1) == 0)
    def _():
        h_scr[...] = _norm_mod(x_ref[...], g_ref[...], mod_ref, i * tm, ctx_len, 0).astype(BF16)

    o_ref[...] = _dot(h_scr[...], w_ref[...])


def in_proj(x, gain, mod, w_bf16, ctx_len, tm, tn):
    l, d = x.shape
    n = w_bf16.shape[1]
    return pl.pallas_call(
        functools.partial(_win_kernel, tm=tm, ctx_len=ctx_len),
        grid=(l // tm, n // tn),
        in_specs=[pl.BlockSpec((tm, d), lambda i, j: (i, 0)),
                  pl.BlockSpec((1, d), lambda i, j: (0, 0)),
                  pl.BlockSpec((8, d), lambda i, j: (0, 0)),
                  pl.BlockSpec((d, tn), lambda i, j: (0, j))],
        out_specs=pl.BlockSpec((tm, tn), lambda i, j: (i, j)),
        out_shape=jax.ShapeDtypeStruct((l, n), F32),
        scratch_shapes=[pltpu.VMEM((tm, d), BF16)],
        compiler_params=_cparams(("arbitrary", "arbitrary")),
        name="in_proj",
    )(x, gain.reshape(1, d), mod, w_bf16)


def _chunk_of_step(i, n_ctx, n_tot, reverse):
    if not reverse:
        return i
    return jnp.where(i < n_ctx, n_ctx - 1 - i, n_tot - 1 - (i - n_ctx))


def _head_norm_gate(o, gate):
    return gate * (o * lax.rsqrt(jnp.mean(o * o, axis=-1, keepdims=True) + EPS))


def _ret_kernel(lg_ref, q_ref, k_ref, v_ref, cos_ref, sin_ref, *rest, reverse, final, chunk):
    if final:
        of_ref, g_ref, o_ref, st_scr = rest
    else:
        o_ref, st_scr = rest
    c = chunk
    lg = lg_ref[1 if reverse else 0, pl.program_id(0)]

    @pl.when(pl.program_id(1) == 0)
    def _():
        st_scr[...] = jnp.zeros_like(st_scr)

    cos2 = cos_ref[...]
    sin2 = sin_ref[...]

    def rope(x):
        return x * cos2 + pltpu.roll(x, HEAD_DIM // 2, 1) * sin2

    q = rope(q_ref[...]).astype(BF16)
    k = rope(k_ref[...]) * (HEAD_DIM ** -0.5)
    v = v_ref[...]
    t = lax.broadcasted_iota(jnp.int32, (c, c), 0)
    s = lax.broadcasted_iota(jnp.int32, (c, c), 1)
    dist = (s - t) if reverse else (t - s)
    dec = jnp.where(dist >= 0, jnp.exp(lg * jnp.maximum(dist, 0).astype(F32)), 0.0)
    p = (_dot_nt(q, k.astype(BF16)) * dec).astype(BF16)
    a = lax.broadcasted_iota(jnp.int32, (c, 1), 0)
    steps_q = (c - a) if reverse else (a + 1)
    steps_k = a if reverse else (c - 1 - a)
    st = st_scr[...]
    o = _dot(p, v.astype(BF16)) + jnp.exp(lg * steps_q.astype(F32)) * _dot_nt(q, st.astype(BF16))
    kd = (k * jnp.exp(lg * steps_k.astype(F32))).astype(BF16)
    st_scr[...] = jnp.exp(lg * c) * st + _dot(v.T.astype(BF16), kd)
    if final:
        g = g_ref[...]
        o_ref[...] = _head_norm_gate(o + of_ref[...], g * jax.nn.sigmoid(g)).astype(o_ref.dtype)
    else:
        o_ref[...] = o


def retention_dir(z, log_gamma, cos2, sin2, o_fwd, ctx_len, reverse):
    l = z.shape[0]
    c = RET_CHUNK
    n_tot, n_ctx = l // c, ctx_len // c
    hb = RET_W // HEAD_DIM

    def zspec(col):
        return pl.BlockSpec((c, HEAD_DIM), lambda h, i: (_chunk_of_step(i, n_ctx, n_tot, reverse), col + h))

    tspec = pl.BlockSpec((c, HEAD_DIM), lambda h, i: (_chunk_of_step(i, n_ctx, n_tot, reverse), 0))
    ospec = pl.BlockSpec((c, HEAD_DIM), lambda h, i: (_chunk_of_step(i, n_ctx, n_tot, reverse), h))
    final = o_fwd is not None
    in_specs = [pl.BlockSpec(memory_space=pltpu.SMEM), zspec(0), zspec(hb), zspec(2 * hb), tspec, tspec]
    args = [log_gamma, z, z, z, cos2, sin2]
    if final:
        in_specs += [ospec, zspec(3 * hb)]
        args += [o_fwd, z]
    return pl.pallas_call(
        functools.partial(_ret_kernel, reverse=reverse, final=final, chunk=c),
        grid=(RET_H, n_tot),
        in_specs=in_specs,
        out_specs=ospec,
        out_shape=jax.ShapeDtypeStruct((l, RET_W), BF16 if final else F32),
        scratch_shapes=[pltpu.VMEM((HEAD_DIM, HEAD_DIM), F32)],
        compiler_params=_cparams(("arbitrary", "arbitrary")),
        name="retention_bwd" if reverse else "retention_fwd",
    )(*args)


def _hg_kernel(q_ref, fz_ref, v_ref, lb_ref, tri_ref, sel_ref, *rest, reverse, final, chunk, sub):
    if final:
        of_ref, g_ref, o_ref, st_scr, cum_scr, k_scr, p_scr = rest
    else:
        o_ref, st_scr, cum_scr, k_scr, p_scr = rest
    c = chunk

    @pl.when(pl.program_id(1) == 0)
    def _():
        st_scr[...] = jnp.zeros_like(st_scr)

    qr = q_ref[...]
    q = qr * jax.nn.sigmoid(qr)
    lb = lb_ref[0]
    sg = jax.nn.sigmoid(fz_ref[...])
    logf = jnp.log(lb + (1.0 - lb) * sg)
    kk = (1.0 - lb) * (1.0 - sg)
    v = v_ref[...].astype(BF16)
    cum = jnp.dot(tri_ref[...], logf, precision=lax.Precision.HIGHEST, preferred_element_type=F32)
    cum_scr[...] = cum
    k_scr[...] = kk
    row = lax.broadcasted_iota(jnp.int32, (c, 1), 0)
    t = lax.broadcasted_iota(jnp.int32, (c, c), 0)
    s = lax.broadcasted_iota(jnp.int32, (c, c), 1)

    a_mat = jnp.zeros((c, c), F32)
    m = c // 2
    while m >= sub:
        parts = []
        for p in range(c // (2 * m)):
            r = p * 2 * m + (m if reverse else m - 1)
            parts.append(jnp.broadcast_to(cum_scr[pl.ds(r, 1), :], (2 * m, HEAD_DIM)))
        dq = cum - (jnp.concatenate(parts, axis=0) if len(parts) > 1 else parts[0])
        second_half = ((row >> (m.bit_length() - 1)) & 1) == 1
        is_q = jnp.logical_not(second_half) if reverse else second_half
        eq = jnp.where(is_q, jnp.exp(jnp.minimum(dq, 0.0)), 0.0)
        ek = jnp.where(is_q, 0.0, jnp.exp(jnp.minimum(-dq, 0.0)))
        a_l = _dot_nt((q * eq).astype(BF16), (kk * ek).astype(BF16))
        a_mat = a_mat + jnp.where((t >> m.bit_length()) == (s >> m.bit_length()), a_l, 0.0)
        m //= 2

    for si in range(c):
        r0 = (si // sub) * sub
        j = si % sub
        rows = r0 + lax.broadcasted_iota(jnp.int32, (sub, 1), 0)
        keep = (rows <= si) if reverse else (rows >= si)
        e = jnp.exp(jnp.minimum(cum[r0:r0 + sub, :] - cum_scr[pl.ds(si, 1), :], 0.0))
        pv = jnp.where(keep, q[r0:r0 + sub, :] * k_scr[pl.ds(si, 1), :] * e, 0.0)
        p_scr[r0:r0 + sub, j * HEAD_DIM:(j + 1) * HEAD_DIM] = pv.astype(BF16)
    a_diag = _dot(p_scr[...], sel_ref[...])
    pieces = []
    for b in range(c // sub):
        blk = a_diag[b * sub:(b + 1) * sub, :]
        pieces.append(blk if b == 0 else pltpu.roll(blk, b * sub, 1))
    a_mat = a_mat + jnp.concatenate(pieces, axis=0)

    st = st_scr[...]
    o = _dot(a_mat.astype(BF16), v) + _dot_nt((q * jnp.exp(cum)).astype(BF16), st.astype(BF16))
    last = cum_scr[pl.ds(0 if reverse else c - 1, 1), :]
    kd = (kk * jnp.exp(last - cum)).astype(BF16)
    st_scr[...] = st * jnp.exp(last) + _dot(v_ref[...].T.astype(BF16), kd)
    if final:
        o_ref[...] = _head_norm_gate(o + of_ref[...], jax.nn.sigmoid(g_ref[...])).astype(o_ref.dtype)
    else:
        o_ref[...] = o


def hgrn2_dir(z, lb, o_fwd, ctx_len, reverse):
    l = z.shape[0]
    c, sub = HG_CHUNK, HG_SUB
    n_tot, n_ctx = l // c, ctx_len // c
    col0 = (RET_COLS + HY_COLS) // HEAD_DIM
    hb = HG_W // HEAD_DIM
    final = o_fwd is not None
    d = 1 if reverse else 0

    def zspec(col):
        return pl.BlockSpec((c, HEAD_DIM), lambda h, i: (_chunk_of_step(i, n_ctx, n_tot, reverse), col0 + col + h))

    ospec = pl.BlockSpec((c, HEAD_DIM), lambda h, i: (_chunk_of_step(i, n_ctx, n_tot, reverse), h))
    r = np.arange(c)
    tri = (r[None, :] >= r[:, None]) if reverse else (r[None, :] <= r[:, None])
    sel = (np.arange(sub * HEAD_DIM)[:, None] // HEAD_DIM) == np.arange(HEAD_DIM)[None, :]
    in_specs = [zspec(0), zspec((1 + d) * hb), zspec(3 * hb),
                pl.BlockSpec((1, 1, HEAD_DIM), lambda h, i: (d * HG_H + h, 0, 0)),
                pl.BlockSpec((c, c), lambda h, i: (0, 0)),
                pl.BlockSpec((sub * HEAD_DIM, HEAD_DIM), lambda h, i: (0, 0))]
    args = [z, z, z, lb.reshape(2 * HG_H, 1, HEAD_DIM), jnp.asarray(tri, F32), jnp.asarray(sel, BF16)]
    if final:
        in_specs += [ospec, zspec(4 * hb)]
        args += [o_fwd, z]
    return pl.pallas_call(
        functools.partial(_hg_kernel, reverse=reverse, final=final, chunk=c, sub=sub),
        grid=(HG_H, n_tot),
        in_specs=in_specs,
        out_specs=ospec,
        out_shape=jax.ShapeDtypeStruct((l, HG_W), BF16 if final else F32),
        scratch_shapes=[pltpu.VMEM((HEAD_DIM, HEAD_DIM), F32), pltpu.VMEM((c, HEAD_DIM), F32),
                        pltpu.VMEM((c, HEAD_DIM), F32), pltpu.VMEM((c, sub * HEAD_DIM), BF16)],
        compiler_params=_cparams(("arbitrary", "arbitrary")),
        name="hgrn2_bwd" if reverse else "hgrn2_fwd",
    )(*args)


def _wout_kernel(oret_ref, ohy_ref, ohg_ref, w_ref, x_ref, g_ref, mod_ref, xo_ref, h_ref, *, tm, ctx_len):
    i = pl.program_id(0)
    y = (_dot(oret_ref[...], w_ref[0:RET_W, :]) + _dot(ohy_ref[...], w_ref[RET_W:RET_W + HY_W, :])
         + _dot(ohg_ref[...], w_ref[RET_W + HY_W:, :]))
    row = i * tm + lax.broadcasted_iota(jnp.int32, (tm, 1), 0)
    gate = jnp.where(row < ctx_len, mod_ref[0:1, :], mod_ref[1:2, :])
    x = x_ref[...] + gate * y
    xo_ref[...] = x
    h_ref[...] = _norm_mod(x, g_ref[...], mod_ref, i * tm, ctx_len, 2).astype(BF16)


def out_proj(o_ret, o_hy, o_hg, w_bf16, x, gain, mod, ctx_len, tm):
    l, d = x.shape
    return pl.pallas_call(
        functools.partial(_wout_kernel, tm=tm, ctx_len=ctx_len),
        grid=(l // tm,),
        in_specs=[pl.BlockSpec((tm, RET_W), lambda i: (i, 0)),
                  pl.BlockSpec((tm, HY_W), lambda i: (i, 0)),
                  pl.BlockSpec((tm, HG_W), lambda i: (i, 0)),
                  pl.BlockSpec((d, d), lambda i: (0, 0)),
                  pl.BlockSpec((tm, d), lambda i: (i, 0)),
                  pl.BlockSpec((1, d), lambda i: (0, 0)),
                  pl.BlockSpec((8, d), lambda i: (0, 0))],
        out_specs=[pl.BlockSpec((tm, d), lambda i: (i, 0)), pl.BlockSpec((tm, d), lambda i: (i, 0))],
        out_shape=[jax.ShapeDtypeStruct((l, d), F32), jax.ShapeDtypeStruct((l, d), BF16)],
        compiler_params=_cparams(("arbitrary",)),
        name="out_proj",
    )(o_ret, o_hy, o_hg, w_bf16, x, gain.reshape(1, d), mod)


def _router_kernel(h_ref, w_ref, b_ref, idx_ref, wt_ref):
    logits = _dot(h_ref[...], w_ref[...]) + b_ref[...]
    lane = lax.broadcasted_iota(jnp.int32, logits.shape, 1).astype(F32)
    vals, idxs = [], []
    x = logits
    for _ in range(TOP_K):
        mx = jnp.max(x, axis=-1, keepdims=True)
        ix = jnp.min(jnp.where(x == mx, lane, float(LANES)), axis=-1, keepdims=True)
        vals.append(mx)
        idxs.append(ix)
        x = jnp.where(lane == ix, -jnp.inf, x)
    es = [jnp.exp(vv - vals[0]) for vv in vals]
    tot = es[0] + es[1] + es[2] + es[3]
    idx_out = jnp.zeros(logits.shape, F32)
    wt_out = jnp.zeros(logits.shape, F32)
    for r in range(TOP_K):
        idx_out = jnp.where(lane == float(r), idxs[r], idx_out)
        wt_out = jnp.where(lane == float(r), es[r] / tot, wt_out)
    idx_ref[...] = idx_out.astype(jnp.int32)
    wt_ref[...] = wt_out


def router(h2, router_w, router_b, tm):
    n, d = h2.shape
    w = jnp.zeros((d, LANES), BF16).at[:, :N_EXPERTS].set(router_w.astype(BF16))
    b = jnp.full((1, LANES), -jnp.inf, F32).at[0, :N_EXPERTS].set(router_b)
    return pl.pallas_call(
        _router_kernel,
        grid=(n // tm,),
        in_specs=[pl.BlockSpec((tm, d), lambda i: (i, 0)),
                  pl.BlockSpec((d, LANES), lambda i: (0, 0)),
                  pl.BlockSpec((1, LANES), lambda i: (0, 0))],
        out_specs=[pl.BlockSpec((tm, LANES), lambda i: (i, 0)), pl.BlockSpec((tm, LANES), lambda i: (i, 0))],
        out_shape=[jax.ShapeDtypeStruct((n, LANES), jnp.int32), jax.ShapeDtypeStruct((n, LANES), F32)],
        compiler_params=_cparams(("arbitrary",)),
        name="router",
    )(h2, w, b)


def _expert_kernel(blk_e_ref, nused_ref, x_ref, wg_ref, bg_ref, wu_ref, bu_ref, wd_ref, bd_ref, o_ref):
    b = pl.program_id(0)
    f = pl.program_id(1)

    @pl.when(b < nused_ref[0])
    def _():
        x = x_ref[...]
        gate = jnp.minimum(_dot(x, wg_ref[0].astype(BF16)) + bg_ref[0], SWIGLU_LIMIT)
        up = jnp.clip(_dot(x, wu_ref[0].astype(BF16)) + bu_ref[0], -SWIGLU_LIMIT, SWIGLU_LIMIT)
        act = gate * jax.nn.sigmoid(SWIGLU_ALPHA * gate) * (up + 1.0)
        part = _dot(act.astype(BF16), wd_ref[0].astype(BF16))

        @pl.when(f == 0)
        def _():
            o_ref[...] = part + bd_ref[0]

        @pl.when(f != 0)
        def _():
            o_ref[...] += part


def expert_ffn(xs, blk_e, n_used, w_gate, b_gate, w_up, b_up, w_down, b_down):
    n_rows, d = xs.shape
    e, _, dff = w_gate.shape
    nb, nf = n_rows // MOE_TM, dff // MOE_TF

    def blk(b, nu):
        return jnp.minimum(b, nu[0] - 1)

    def fidx(b, f, nu):
        return jnp.where(b < nu[0], f, nf - 1)

    grid_spec = pltpu.PrefetchScalarGridSpec(
        num_scalar_prefetch=2,
        grid=(nb, nf),
        in_specs=[pl.BlockSpec((MOE_TM, d), lambda b, f, be, nu: (blk(b, nu), 0)),
                  pl.BlockSpec((1, d, MOE_TF), lambda b, f, be, nu: (be[blk(b, nu)], 0, fidx(b, f, nu))),
                  pl.BlockSpec((1, 1, MOE_TF), lambda b, f, be, nu: (be[blk(b, nu)], 0, fidx(b, f, nu))),
                  pl.BlockSpec((1, d, MOE_TF), lambda b, f, be, nu: (be[blk(b, nu)], 0, fidx(b, f, nu))),
                  pl.BlockSpec((1, 1, MOE_TF), lambda b, f, be, nu: (be[blk(b, nu)], 0, fidx(b, f, nu))),
                  pl.BlockSpec((1, MOE_TF, d), lambda b, f, be, nu: (be[blk(b, nu)], fidx(b, f, nu), 0)),
                  pl.BlockSpec((1, 1, d), lambda b, f, be, nu: (be[blk(b, nu)], 0, 0))],
        out_specs=pl.BlockSpec((MOE_TM, d), lambda b, f, be, nu: (blk(b, nu), 0)),
    )
    return pl.pallas_call(
        _expert_kernel,
        grid_spec=grid_spec,
        out_shape=jax.ShapeDtypeStruct((n_rows, d), F32),
        compiler_params=_cparams(("arbitrary", "arbitrary")),
        name="expert_ffn",
    )(blk_e, n_used, xs, w_gate, b_gate.reshape(e, 1, dff), w_up, b_up.reshape(e, 1, dff),
      w_down, b_down.reshape(e, 1, d))


def _rope_tables(ctx_len, n_rows):
    nf = HEAD_DIM // 4
    inv = ROPE_BASE ** (-jnp.arange(nf, dtype=F32) / nf)
    rows = jnp.repeat(jnp.arange(n_rows, dtype=F32), GRID_W)
    cols = (jnp.arange(n_rows * GRID_W) % GRID_W).astype(F32)
    ang = jnp.concatenate([rows[:, None] * inv, cols[:, None] * inv], axis=-1)
    ang = jnp.concatenate([jnp.zeros((ctx_len, HEAD_DIM // 2), F32), ang], axis=0)
    cos_t, sin_t = jnp.cos(ang), jnp.sin(ang)
    return jnp.concatenate([cos_t, cos_t], axis=-1), jnp.concatenate([-sin_t, sin_t], axis=-1)


def _dwconv3(u, w, b):
    ch = u.shape[-1]
    y = lax.conv_general_dilated(u, w[:, None, :].astype(u.dtype), window_strides=(1,), padding=((1, 1),),
                                 dimension_numbers=('NWC', 'WIO', 'NWC'), feature_group_count=ch)
    return y + b


def _hyena_filters(L, w1, b1, freq, w2, b2, w3):
    t = jnp.linspace(0.0, 1.0, L, dtype=F32)[:, None]
    w = 2.0 * math.pi * jnp.arange(L, dtype=F32)[:, None] / L
    f = jnp.linspace(1e-4, HY_BANDS - 1, HY_BANDS, dtype=F32)[None, :]
    feats = jnp.concatenate([t, jnp.cos(f * w), -jnp.sin(f * w)], axis=-1)
    h = jnp.sin(freq * (feats @ w1 + b1))
    h = jnp.sin(freq * (h @ w2 + b2))
    h = (h @ w3).astype(F32)
    deltas = jnp.abs(jnp.linspace(HY_MIN_DECAY, HY_MAX_DECAY, HY_W, dtype=F32))
    window = jnp.exp(-t * deltas[None, :])
    return h.reshape(L, 2, 2, HY_W) * window[:, None, None, :]


def _long_conv_bidir(u, h_fwd, h_bwd, skip):
    L, ch = h_fwd.shape
    filt = jnp.concatenate([h_fwd, jnp.zeros((1, ch), h_fwd.dtype), jnp.flip(h_bwd[1:], axis=0)], axis=0)
    spec = jnp.fft.rfft(u, n=2 * L, axis=1) * jnp.fft.rfft(filt, n=2 * L, axis=0)[None]
    return jnp.fft.irfft(spec, n=2 * L, axis=1)[:, :L] + skip * u


def _hyena_seq(u, conv_w, conv_b, w1, b1, freq, w2, b2, w3, skip):
    L = u.shape[1]
    v, x1, x2 = jnp.split(_dwconv3(u, conv_w, conv_b).astype(F32), 3, axis=-1)
    h = _hyena_filters(L, w1, b1, freq, w2, b2, w3)
    y = x1 * _long_conv_bidir(v, h[:, 0, 0], h[:, 0, 1], skip[0])
    return x2 * _long_conv_bidir(y, h[:, 1, 0], h[:, 1, 1], skip[1])


def _moe(h2, x, gate2, layer_p, n_tok):
    router_w, router_b, w_gate, b_gate, w_up, b_up, w_down, b_down = layer_p
    l, d = x.shape
    top_idx, top_w = router(h2, router_w, router_b, tm=256)
    top_idx, top_w = top_idx[:n_tok, :TOP_K], top_w[:n_tok, :TOP_K]
    onehot = (top_idx[:, :, None] == jnp.arange(N_EXPERTS)[None, None, :]).astype(jnp.int32)
    per_tok = onehot.sum(axis=1)
    before = jnp.cumsum(per_tok, axis=0) - per_tok
    counts = per_tok.sum(axis=0)
    nblk = (counts + MOE_TM - 1) // MOE_TM
    blk_end = jnp.cumsum(nblk)
    pad_start = (blk_end - nblk) * MOE_TM
    rank = jnp.take_along_axis(before, top_idx, axis=1)
    dest = pad_start[top_idx] + rank
    n_blocks = -(-(n_tok * TOP_K + N_EXPERTS * (MOE_TM - 1)) // MOE_TM)
    blk_e = jnp.minimum(jnp.searchsorted(blk_end, jnp.arange(n_blocks), side='right'), N_EXPERTS - 1)
    n_used = blk_end[-1:].astype(jnp.int32)
    tok = jnp.broadcast_to(jnp.arange(n_tok, dtype=jnp.int32)[:, None], dest.shape)
    slot_tok = jnp.zeros((n_blocks * MOE_TM,), jnp.int32).at[dest.reshape(-1)].set(tok.reshape(-1))
    xs = h2[slot_tok]
    ys = expert_ffn(xs, blk_e.astype(jnp.int32), n_used, w_gate, b_gate, w_up, b_up, w_down, b_down)
    contrib = ys[dest.reshape(-1)].reshape(n_tok, TOP_K, d) * top_w[:, :, None]
    out = contrib.sum(axis=1)
    if n_tok < l:
        out = jnp.concatenate([out, jnp.zeros((l - n_tok, d), F32)], axis=0)
    return x + gate2 * out


def kernel(x, c, ctx, c_ctx, norm_mix_g, norm_ffn_g, final_norm_g, ada_w, ada_b, w_in, w_out, ret_decay_logit,
           hy_conv_w, hy_conv_b, hy_filt_w1, hy_filt_b1, hy_filt_freq, hy_filt_w2, hy_filt_b2, hy_filt_w3, hy_skip,
           hg_lb_logits, router_w, router_b, moe_w_gate, moe_b_gate, moe_w_up, moe_b_up, moe_w_down, moe_b_down):
    bsz, seq, d = x.shape
    assert bsz == 1
    ctx_len = ctx.shape[1]
    l = ctx_len + seq
    cos2, sin2 = _rope_tables(ctx_len, seq // GRID_W)
    cvec = jnp.zeros((8, d), F32).at[0].set(c_ctx).at[1].set(c[0])
    mod = adaln(cvec, ada_w, ada_b)
    xs = jnp.concatenate([ctx[0], x[0]], axis=0)
    p_lb = jax.nn.softmax(hg_lb_logits.astype(F32), axis=0)
    lb_cum = jnp.cumsum(p_lb, axis=0)
    is_ctx = (jnp.arange(l) < ctx_len)[:, None]
    for layer in range(DEPTH):
        last = layer == DEPTH - 1
        m = mod[layer].reshape(8, N_MOD, d)
        mc, ml = m[0], m[1]
        mod_in = jnp.stack([mc[0], mc[1], ml[0], ml[1], mc[0], mc[0], mc[0], mc[0]])
        z = in_proj(xs, norm_mix_g[layer], mod_in, w_in[layer].astype(BF16), ctx_len, tm=768, tn=768)
        log_gamma = jax.nn.log_sigmoid(ret_decay_logit[layer].astype(F32))
        o_ret = retention_dir(z, log_gamma, cos2, sin2, None, ctx_len, reverse=False)
        o_ret = retention_dir(z, log_gamma, cos2, sin2, o_ret, ctx_len, reverse=True)
        lb = (lb_cum[layer] - p_lb[0]).reshape(2, HG_H, 1, HEAD_DIM)
        o_hg = hgrn2_dir(z, lb, None, ctx_len, reverse=False)
        o_hg = hgrn2_dir(z, lb, o_hg, ctx_len, reverse=True)
        hy_p = (hy_conv_w[layer], hy_conv_b[layer], hy_filt_w1[layer], hy_filt_b1[layer], hy_filt_freq[layer],
                hy_filt_w2[layer], hy_filt_b2[layer], hy_filt_w3[layer], hy_skip[layer])
        z_hy = z[:, RET_COLS:RET_COLS + HY_COLS]
        o_hy_l = _hyena_seq(z_hy[None, ctx_len:], *hy_p)[0]
        if last:
            o_hy_c = jnp.zeros((ctx_len, HY_W), F32)
        else:
            o_hy_c = _hyena_seq(z_hy[None, :ctx_len], *hy_p)[0]
        o_hy = jnp.concatenate([o_hy_c, o_hy_l], axis=0).astype(BF16)
        mod_out = jnp.stack([mc[2], ml[2], mc[3], mc[4], ml[3], ml[4], mc[0], mc[0]])
        xs, h2 = out_proj(o_ret, o_hy, o_hg, w_out[layer].astype(BF16), xs, norm_ffn_g[layer], mod_out, ctx_len,
                          tm=384)
        moe_p = (router_w[layer], router_b[layer], moe_w_gate[layer], moe_b_gate[layer], moe_w_up[layer],
                 moe_b_up[layer], moe_w_down[layer], moe_b_down[layer])
        gate2 = jnp.where(is_ctx, mc[5][None, :], ml[5][None, :])
        if last:
            xs_l = _moe(h2[ctx_len:], xs[ctx_len:], ml[5][None, :], moe_p, seq)
            xs = jnp.concatenate([xs[:ctx_len], xs_l], axis=0)
        else:
            xs = _moe(h2, xs, gate2, moe_p, l)
    xl = xs[ctx_len:]
    y = xl * lax.rsqrt(jnp.mean(xl * xl, axis=-1, keepdims=True) + EPS) * final_norm_g
    return y[None]
```

```python
import functools
import math

import jax
import jax.numpy as jnp
import numpy as np
from jax import lax
from jax.experimental import pallas as pl
from jax.experimental.pallas import tpu as pltpu

F32 = jnp.float32
BF16 = jnp.bfloat16

D_MODEL = 2048
DEPTH = 2
GRID_W = 64
HEAD_DIM = 128
RET_W = 3 * D_MODEL // 8
RET_H = RET_W // HEAD_DIM
HY_W = D_MODEL // 4
HG_W = D_MODEL - RET_W - HY_W
HG_H = HG_W // HEAD_DIM
RET_COLS = 4 * RET_W
HY_COLS = 3 * HY_W
HG_COLS = 5 * HG_W
IN_W = RET_COLS + HY_COLS + HG_COLS
ROPE_BASE = 10000.0
EPS = 1e-6
N_MOD = 6
HY_BANDS = 16
HY_MAX_DECAY = math.log(1e-2) / 0.3
HY_MIN_DECAY = math.log(1e-2) / 1.5
N_EXPERTS = 32
TOP_K = 4
SWIGLU_LIMIT = 7.0
SWIGLU_ALPHA = 1.702

LANES = 128
VMEM_LIMIT = 56 * 1024 * 1024

RET_CHUNK = 256
HG_CHUNK = 128
HG_SUB = 16
MOE_TM = 512
MOE_TF = 512


def _cparams(sem):
    return pltpu.CompilerParams(dimension_semantics=sem, vmem_limit_bytes=VMEM_LIMIT)


def _dot(a, b):
    return jnp.dot(a, b, preferred_element_type=F32)


def _dot_nt(a, b):
    return lax.dot_general(a, b, (((1,), (1,)), ((), ())), preferred_element_type=F32)


def _adaln_kernel(c_ref, w_ref, b_ref, o_ref):
    c = c_ref[...]
    cond = c * jax.nn.sigmoid(c)
    o_ref[0] = _dot(cond.astype(BF16), w_ref[0].astype(BF16)) + b_ref[0]


def adaln(cvec, ada_w, ada_b, tn=1536):
    depth, d, n = ada_w.shape
    return pl.pallas_call(
        _adaln_kernel,
        grid=(depth, n // tn),
        in_specs=[pl.BlockSpec((8, d), lambda l, j: (0, 0)),
                  pl.BlockSpec((1, d, tn), lambda l, j: (l, 0, j)),
                  pl.BlockSpec((1, 1, tn), lambda l, j: (l, 0, j))],
        out_specs=pl.BlockSpec((1, 8, tn), lambda l, j: (l, 0, j)),
        out_shape=jax.ShapeDtypeStruct((depth, 8, n), F32),
        compiler_params=_cparams(("arbitrary", "arbitrary")),
        name="adaln",
    )(cvec, ada_w, ada_b.reshape(depth, 1, n))


def _norm_mod(x, gain, mod_ref, row0, ctx_len, shift_row):
    y = x * lax.rsqrt(jnp.mean(x * x, axis=-1, keepdims=True) + EPS) * gain
    row = row0 + lax.broadcasted_iota(jnp.int32, (x.shape[0], 1), 0)
    is_ctx = row < ctx_len
    shift = jnp.where(is_ctx, mod_ref[shift_row:shift_row + 1, :], mod_ref[shift_row + 2:shift_row + 3, :])
    scale = jnp.where(is_ctx, mod_ref[shift_row + 1:shift_row + 2, :], mod_ref[shift_row + 3:shift_row + 4, :])
    return y * (1.0 + scale) + shift


def _win_kernel(x_ref, g_ref, mod_ref, w_ref, o_ref, h_scr, *, tm, ctx_len):
    i = pl.program_id(0)

    @pl.when(pl.program_id(1) == 0)
    def _():
        h_scr[...] = _norm_mod(x_ref[...], g_ref[...], mod_ref, i * tm, ctx_len, 0).astype(BF16)

    o_ref[...] = _dot(h_scr[...], w_ref[...])


def in_proj(x, gain, mod, w_bf16, ctx_len, tm, tn):
    l, d = x.shape
    n = w_bf16.shape[1]
    return pl.pallas_call(
        functools.partial(_win_kernel, tm=tm, ctx_len=ctx_len),
        grid=(l // tm, n // tn),
        in_specs=[pl.BlockSpec((tm, d), lambda i, j: (i, 0)),
                  pl.BlockSpec((1, d), lambda i, j: (0, 0)),
                  pl.BlockSpec((8, d), lambda i, j: (0, 0)),
                  pl.BlockSpec((d, tn), lambda i, j: (0, j))],
        out_specs=pl.BlockSpec((tm, tn), lambda i, j: (i, j)),
        out_shape=jax.ShapeDtypeStruct((l, n), F32),
        scratch_shapes=[pltpu.VMEM((tm, d), BF16)],
        compiler_params=_cparams(("arbitrary", "arbitrary")),
        name="in_proj",
    )(x, gain.reshape(1, d), mod, w_bf16)


def _chunk_of_step(i, n_ctx, n_tot, reverse):
    if not reverse:
        return i
    return jnp.where(i < n_ctx, n_ctx - 1 - i, n_tot - 1 - (i - n_ctx))


def _head_norm_gate(o, gate):
    return gate * (o * lax.rsqrt(jnp.mean(o * o, axis=-1, keepdims=True) + EPS))


def _ret_kernel(lg_ref, q_ref, k_ref, v_ref, cos_ref, sin_ref, *rest, reverse, final, chunk):
    if final:
        of_ref, g_ref, o_ref, st_scr = rest
    else:
        o_ref, st_scr = rest
    c = chunk
    lg = lg_ref[1 if reverse else 0, pl.program_id(0)]

    @pl.when(pl.program_id(1) == 0)
    def _():
        st_scr[...] = jnp.zeros_like(st_scr)

    cos2 = cos_ref[...]
    sin2 = sin_ref[...]

    def rope(x):
        return x * cos2 + pltpu.roll(x, HEAD_DIM // 2, 1) * sin2

    q = rope(q_ref[...]).astype(BF16)
    k = rope(k_ref[...]) * (HEAD_DIM ** -0.5)
    v = v_ref[...]
    t = lax.broadcasted_iota(jnp.int32, (c, c), 0)
    s = lax.broadcasted_iota(jnp.int32, (c, c), 1)
    dist = (s - t) if reverse else (t - s)
    dec = jnp.where(dist >= 0, jnp.exp(lg * jnp.maximum(dist, 0).astype(F32)), 0.0)
    p = (_dot_nt(q, k.astype(BF16)) * dec).astype(BF16)
    a = lax.broadcasted_iota(jnp.int32, (c, 1), 0)
    steps_q = (c - a) if reverse else (a + 1)
    steps_k = a if reverse else (c - 1 - a)
    st = st_scr[...]
    o = _dot(p, v.astype(BF16)) + jnp.exp(lg * steps_q.astype(F32)) * _dot_nt(q, st.astype(BF16))
    kd = (k * jnp.exp(lg * steps_k.astype(F32))).astype(BF16)
    st_scr[...] = jnp.exp(lg * c) * st + _dot(v.T.astype(BF16), kd)
    if final:
        g = g_ref[...]
        o_ref[...] = _head_norm_gate(o + of_ref[...], g * jax.nn.sigmoid(g)).astype(o_ref.dtype)
    else:
        o_ref[...] = o


def retention_dir(z, log_gamma, cos2, sin2, o_fwd, ctx_len, reverse):
    l = z.shape[0]
    c = RET_CHUNK
    n_tot, n_ctx = l // c, ctx_len // c
    hb = RET_W // HEAD_DIM

    def zspec(col):
        return pl.BlockSpec((c, HEAD_DIM), lambda h, i: (_chunk_of_step(i, n_ctx, n_tot, reverse), col + h))

    tspec = pl.BlockSpec((c, HEAD_DIM), lambda h, i: (_chunk_of_step(i, n_ctx, n_tot, reverse), 0))
    ospec = pl.BlockSpec((c, HEAD_DIM), lambda h, i: (_chunk_of_step(i, n_ctx, n_tot, reverse), h))
    final = o_fwd is not None
    in_specs = [pl.BlockSpec(memory_space=pltpu.SMEM), zspec(0), zspec(hb), zspec(2 * hb), tspec, tspec]
    args = [log_gamma, z, z, z, cos2, sin2]
    if final:
        in_specs += [ospec, zspec(3 * hb)]
        args += [o_fwd, z]
    return pl.pallas_call(
        functools.partial(_ret_kernel, reverse=reverse, final=final, chunk=c),
        grid=(RET_H, n_tot),
        in_specs=in_specs,
        out_specs=ospec,
        out_shape=jax.ShapeDtypeStruct((l, RET_W), BF16 if final else F32),
        scratch_shapes=[pltpu.VMEM((HEAD_DIM, HEAD_DIM), F32)],
        compiler_params=_cparams(("arbitrary", "arbitrary")),
        name="retention_bwd" if reverse else "retention_fwd",
    )(*args)


def _hg_kernel(q_ref, fz_ref, v_ref, lb_ref, tri_ref, sel_ref, *rest, reverse, final, chunk, sub):
    if final:
        of_ref, g_ref, o_ref, st_scr, cum_scr, k_scr, p_scr = rest
    else:
        o_ref, st_scr, cum_scr, k_scr, p_scr = rest
    c = chunk

    @pl.when(pl.program_id(1) == 0)
    def _():
        st_scr[...] = jnp.zeros_like(st_scr)

    qr = q_ref[...]
    q = qr * jax.nn.sigmoid(qr)
    lb = lb_ref[0]
    sg = jax.nn.sigmoid(fz_ref[...])
    logf = jnp.log(lb + (1.0 - lb) * sg)
    kk = (1.0 - lb) * (1.0 - sg)
    v = v_ref[...].astype(BF16)
    cum = jnp.dot(tri_ref[...], logf, precision=lax.Precision.HIGHEST, preferred_element_type=F32)
    cum_scr[...] = cum
    k_scr[...] = kk
    row = lax.broadcasted_iota(jnp.int32, (c, 1), 0)
    t = lax.broadcasted_iota(jnp.int32, (c, c), 0)
    s = lax.broadcasted_iota(jnp.int32, (c, c), 1)

    a_mat = jnp.zeros((c, c), F32)
    m = c // 2
    while m >= sub:
        parts = []
        for p in range(c // (2 * m)):
            r = p * 2 * m + (m if reverse else m - 1)
            parts.append(jnp.broadcast_to(cum_scr[pl.ds(r, 1), :], (2 * m, HEAD_DIM)))
        dq = cum - (jnp.concatenate(parts, axis=0) if len(parts) > 1 else parts[0])
        second_half = ((row >> (m.bit_length() - 1)) & 1) == 1
        is_q = jnp.logical_not(second_half) if reverse else second_half
        eq = jnp.where(is_q, jnp.exp(jnp.minimum(dq, 0.0)), 0.0)
        ek = jnp.where(is_q, 0.0, jnp.exp(jnp.minimum(-dq, 0.0)))
        a_l = _dot_nt((q * eq).astype(BF16), (kk * ek).astype(BF16))
        a_mat = a_mat + jnp.where((t >> m.bit_length()) == (s >> m.bit_length()), a_l, 0.0)
        m //= 2

    for si in range(c):
        r0 = (si // sub) * sub
        j = si % sub
        rows = r0 + lax.broadcasted_iota(jnp.int32, (sub, 1), 0)
        keep = (rows <= si) if reverse else (rows >= si)
        e = jnp.exp(jnp.minimum(cum[r0:r0 + sub, :] - cum_scr[pl.ds(si, 1), :], 0.0))
        pv = jnp.where(keep, q[r0:r0 + sub, :] * k_scr[pl.ds(si, 1), :] * e, 0.0)
        p_scr[r0:r0 + sub, j * HEAD_DIM:(j + 1) * HEAD_DIM] = pv.astype(BF16)
    a_diag = _dot(p_scr[...], sel_ref[...])
    pieces = []
    for b in range(c // sub):
        blk = a_diag[b * sub:(b + 1) * sub, :]
        pieces.append(blk if b == 0 else pltpu.roll(blk, b * sub, 1))
    a_mat = a_mat + jnp.concatenate(pieces, axis=0)

    st = st_scr[...]
    o = _dot(a_mat.astype(BF16), v) + _dot_nt((q * jnp.exp(cum)).astype(BF16), st.astype(BF16))
    last = cum_scr[pl.ds(0 if reverse else c - 1, 1), :]
    kd = (kk * jnp.exp(last - cum)).astype(BF16)
    st_scr[...] = st * jnp.exp(last) + _dot(v_ref[...].T.astype(BF16), kd)
    if final:
        o_ref[...] = _head_norm_gate(o + of_ref[...], jax.nn.sigmoid(g_ref[...])).astype(o_ref.dtype)
    else:
        o_ref[...] = o


def hgrn2_dir(z, lb, o_fwd, ctx_len, reverse):
    l = z.shape[0]
    c, sub = HG_CHUNK, HG_SUB
    n_tot, n_ctx = l // c, ctx_len // c
    col0 = (RET_COLS + HY_COLS) // HEAD_DIM
    hb = HG_W // HEAD_DIM
    final = o_fwd is not None
    d = 1 if reverse else 0

    def zspec(col):
        return pl.BlockSpec((c, HEAD_DIM), lambda h, i: (_chunk_of_step(i, n_ctx, n_tot, reverse), col0 + col + h))

    ospec = pl.BlockSpec((c, HEAD_DIM), lambda h, i: (_chunk_of_step(i, n_ctx, n_tot, reverse), h))
    r = np.arange(c)
    tri = (r[None, :] >= r[:, None]) if reverse else (r[None, :] <= r[:, None])
    sel = (np.arange(sub * HEAD_DIM)[:, None] // HEAD_DIM) == np.arange(HEAD_DIM)[None, :]
    in_specs = [zspec(0), zspec((1 + d) * hb), zspec(3 * hb),
                pl.BlockSpec((1, 1, HEAD_DIM), lambda h, i: (d * HG_H + h, 0, 0)),
                pl.BlockSpec((c, c), lambda h, i: (0, 0)),
                pl.BlockSpec((sub * HEAD_DIM, HEAD_DIM), lambda h, i: (0, 0))]
    args = [z, z, z, lb.reshape(2 * HG_H, 1, HEAD_DIM), jnp.asarray(tri, F32), jnp.asarray(sel, BF16)]
    if final:
        in_specs += [ospec, zspec(4 * hb)]
        args += [o_fwd, z]
    return pl.pallas_call(
        functools.partial(_hg_kernel, reverse=reverse, final=final, chunk=c, sub=sub),
        grid=(HG_H, n_tot),
        in_specs=in_specs,
        out_specs=ospec,
        out_shape=jax.ShapeDtypeStruct((l, HG_W), BF16 if final else F32),
        scratch_shapes=[pltpu.VMEM((HEAD_DIM, HEAD_DIM), F32), pltpu.VMEM((c, HEAD_DIM), F32),
                        pltpu.VMEM((c, HEAD_DIM), F32), pltpu.VMEM((c, sub * HEAD_DIM), BF16)],
        compiler_params=_cparams(("arbitrary", "arbitrary")),
        name="hgrn2_bwd" if reverse else "hgrn2_fwd",
    )(*args)


def _wout_kernel(oret_ref, ohy_ref, ohg_ref, w_ref, x_ref, g_ref, mod_ref, xo_ref, h_ref, *, tm, ctx_len):
    i = pl.program_id(0)
    y = (_dot(oret_ref[...], w_ref[0:RET_W, :]) + _dot(ohy_ref[...].astype(BF16), w_ref[RET_W:RET_W + HY_W, :])
         + _dot(ohg_ref[...], w_ref[RET_W + HY_W:, :]))
    row = i * tm + lax.broadcasted_iota(jnp.int32, (tm, 1), 0)
    gate = jnp.where(row < ctx_len, mod_ref[0:1, :], mod_ref[1:2, :])
    x = x_ref[...] + gate * y
    xo_ref[...] = x
    h_ref[...] = _norm_mod(x, g_ref[...], mod_ref, i * tm, ctx_len, 2)


def out_proj(o_ret, o_hy, o_hg, w_bf16, x, gain, mod, ctx_len, tm):
    l, d = x.shape
    return pl.pallas_call(
        functools.partial(_wout_kernel, tm=tm, ctx_len=ctx_len),
        grid=(l // tm,),
        in_specs=[pl.BlockSpec((tm, RET_W), lambda i: (i, 0)),
                  pl.BlockSpec((tm, HY_W), lambda i: (i, 0)),
                  pl.BlockSpec((tm, HG_W), lambda i: (i, 0)),
                  pl.BlockSpec((d, d), lambda i: (0, 0)),
                  pl.BlockSpec((tm, d), lambda i: (i, 0)),
                  pl.BlockSpec((1, d), lambda i: (0, 0)),
                  pl.BlockSpec((8, d), lambda i: (0, 0))],
        out_specs=[pl.BlockSpec((tm, d), lambda i: (i, 0)), pl.BlockSpec((tm, d), lambda i: (i, 0))],
        out_shape=[jax.ShapeDtypeStruct((l, d), F32), jax.ShapeDtypeStruct((l, d), F32)],
        compiler_params=_cparams(("arbitrary",)),
        name="out_proj",
    )(o_ret, o_hy, o_hg, w_bf16, x, gain.reshape(1, d), mod)


def _router_kernel(h_ref, w_ref, b_ref, idx_ref, wt_ref):
    logits = _dot(h_ref[...].astype(BF16), w_ref[...]) + b_ref[...]
    lane = lax.broadcasted_iota(jnp.int32, logits.shape, 1).astype(F32)
    vals, idxs = [], []
    x = logits
    for _ in range(TOP_K):
        mx = jnp.max(x, axis=-1, keepdims=True)
        ix = jnp.min(jnp.where(x == mx, lane, float(LANES)), axis=-1, keepdims=True)
        vals.append(mx)
        idxs.append(ix)
        x = jnp.where(lane == ix, -jnp.inf, x)
    es = [jnp.exp(vv - vals[0]) for vv in vals]
    tot = es[0] + es[1] + es[2] + es[3]
    idx_out = jnp.zeros(logits.shape, F32)
    wt_out = jnp.zeros(logits.shape, F32)
    for r in range(TOP_K):
        idx_out = jnp.where(lane == float(r), idxs[r], idx_out)
        wt_out = jnp.where(lane == float(r), es[r] / tot, wt_out)
    idx_ref[...] = idx_out.astype(jnp.int32)
    wt_ref[...] = wt_out


def router(h2, router_w, router_b, row0, n_tok, tm=256):
    d = h2.shape[1]
    w = jnp.zeros((d, LANES), BF16).at[:, :N_EXPERTS].set(router_w.astype(BF16))
    b = jnp.full((1, LANES), -jnp.inf, F32).at[0, :N_EXPERTS].set(router_b)
    return pl.pallas_call(
        _router_kernel,
        grid=(n_tok // tm,),
        in_specs=[pl.BlockSpec((tm, d), lambda i: (i + row0 // tm, 0)),
                  pl.BlockSpec((d, LANES), lambda i: (0, 0)),
                  pl.BlockSpec((1, LANES), lambda i: (0, 0))],
        out_specs=[pl.BlockSpec((tm, LANES), lambda i: (i, 0)), pl.BlockSpec((tm, LANES), lambda i: (i, 0))],
        out_shape=[jax.ShapeDtypeStruct((n_tok, LANES), jnp.int32), jax.ShapeDtypeStruct((n_tok, LANES), F32)],
        compiler_params=_cparams(("arbitrary",)),
        name="router",
    )(h2, w, b)


def _row_copy(src_hbm, src_row, dst, dst_row, sem):
    return pltpu.make_async_copy(src_hbm.at[pl.ds(src_row, 1)], dst.at[pl.ds(dst_row, 1)], sem)


def _expert_kernel(blk_e_ref, nused_ref, tok_ref, h_hbm, wg_ref, bg_ref, wu_ref, bu_ref, wd_ref, bd_ref, o_ref,
                   xf_scr, xb_scr, sem, *, row0):
    b = pl.program_id(0)
    f = pl.program_id(1)

    @pl.when(b < nused_ref[0])
    def _():
        @pl.when(f == 0)
        def _():
            def start(r, carry):
                _row_copy(h_hbm, row0 + tok_ref[b * MOE_TM + r], xf_scr, r, sem).start()
                return carry

            def wait(r, carry):
                _row_copy(h_hbm, 0, xf_scr, r, sem).wait()
                return carry

            lax.fori_loop(0, MOE_TM, start, 0, unroll=8)
            lax.fori_loop(0, MOE_TM, wait, 0, unroll=8)
            xb_scr[...] = xf_scr[...].astype(BF16)

        x = xb_scr[...]
        gate = jnp.minimum(_dot(x, wg_ref[0].astype(BF16)) + bg_ref[0], SWIGLU_LIMIT)
        up = jnp.clip(_dot(x, wu_ref[0].astype(BF16)) + bu_ref[0], -SWIGLU_LIMIT, SWIGLU_LIMIT)
        act = gate * jax.nn.sigmoid(SWIGLU_ALPHA * gate) * (up + 1.0)
        part = _dot(act.astype(BF16), wd_ref[0].astype(BF16))

        @pl.when(f == 0)
        def _():
            o_ref[...] = part + bd_ref[0]

        @pl.when(f != 0)
        def _():
            o_ref[...] += part

    @pl.when((b >= nused_ref[0]) & (f == 0))
    def _():
        o_ref[...] = jnp.zeros_like(o_ref)


def expert_ffn(h2, row0, slot_tok, blk_e, n_used, w_gate, b_gate, w_up, b_up, w_down, b_down):
    d = h2.shape[1]
    n_rows = slot_tok.shape[0]
    e, _, dff = w_gate.shape
    nb, nf = n_rows // MOE_TM, dff // MOE_TF

    def blk(b, nu):
        return jnp.minimum(b, nu[0] - 1)

    def fidx(b, f, nu):
        return jnp.where(b < nu[0], f, nf - 1)

    grid_spec = pltpu.PrefetchScalarGridSpec(
        num_scalar_prefetch=3,
        grid=(nb, nf),
        in_specs=[pl.BlockSpec(memory_space=pl.ANY),
                  pl.BlockSpec((1, d, MOE_TF), lambda b, f, be, nu, tk: (be[blk(b, nu)], 0, fidx(b, f, nu))),
                  pl.BlockSpec((1, 1, MOE_TF), lambda b, f, be, nu, tk: (be[blk(b, nu)], 0, fidx(b, f, nu))),
                  pl.BlockSpec((1, d, MOE_TF), lambda b, f, be, nu, tk: (be[blk(b, nu)], 0, fidx(b, f, nu))),
                  pl.BlockSpec((1, 1, MOE_TF), lambda b, f, be, nu, tk: (be[blk(b, nu)], 0, fidx(b, f, nu))),
                  pl.BlockSpec((1, MOE_TF, d), lambda b, f, be, nu, tk: (be[blk(b, nu)], fidx(b, f, nu), 0)),
                  pl.BlockSpec((1, 1, d), lambda b, f, be, nu, tk: (be[blk(b, nu)], 0, 0))],
        out_specs=pl.BlockSpec((MOE_TM, d), lambda b, f, be, nu, tk: (b, 0)),
        scratch_shapes=[pltpu.VMEM((MOE_TM, d), F32), pltpu.VMEM((MOE_TM, d), BF16), pltpu.SemaphoreType.DMA],
    )
    return pl.pallas_call(
        functools.partial(_expert_kernel, row0=row0),
        grid_spec=grid_spec,
        out_shape=jax.ShapeDtypeStruct((n_rows, d), F32),
        compiler_params=_cparams(("arbitrary", "arbitrary")),
        name="expert_ffn",
    )(blk_e, n_used, slot_tok, h2, w_gate, b_gate.reshape(e, 1, dff), w_up, b_up.reshape(e, 1, dff),
      w_down, b_down.reshape(e, 1, d))


def _combine_kernel(dest_ref, y_hbm, wt_ref, x_ref, mod_ref, g_ref, o_ref, y_scr, sem, *, tm, row0, ctx_len, final):
    i = pl.program_id(0)

    def start(r, carry):
        for k in range(TOP_K):
            _row_copy(y_hbm, dest_ref[(i * tm + r) * TOP_K + k], y_scr.at[k], r, sem).start()
        return carry

    def wait(r, carry):
        for k in range(TOP_K):
            _row_copy(y_hbm, 0, y_scr.at[k], r, sem).wait()
        return carry

    lax.fori_loop(0, tm, start, 0, unroll=4)
    lax.fori_loop(0, tm, wait, 0, unroll=4)
    wt = wt_ref[...]
    acc = y_scr[0] * wt[:, 0:1]
    for k in range(1, TOP_K):
        acc = acc + y_scr[k] * wt[:, k:k + 1]
    row = row0 + i * tm + lax.broadcasted_iota(jnp.int32, (tm, 1), 0)
    x = x_ref[...] + jnp.where(row < ctx_len, mod_ref[0:1, :], mod_ref[1:2, :]) * acc
    if final:
        x = x * lax.rsqrt(jnp.mean(x * x, axis=-1, keepdims=True) + EPS) * g_ref[...]
    o_ref[...] = x


def moe_combine(ys, dest, top_w, x, mod, final_gain, row0, n_tok, ctx_len, final, tm=128):
    d = x.shape[1]
    grid_spec = pltpu.PrefetchScalarGridSpec(
        num_scalar_prefetch=1,
        grid=(n_tok // tm,),
        in_specs=[pl.BlockSpec(memory_space=pl.ANY),
                  pl.BlockSpec((tm, LANES), lambda i, ds: (i, 0)),
                  pl.BlockSpec((tm, d), lambda i, ds: (i + row0 // tm, 0)),
                  pl.BlockSpec((8, d), lambda i, ds: (0, 0)),
                  pl.BlockSpec((1, d), lambda i, ds: (0, 0))],
        out_specs=pl.BlockSpec((tm, d), lambda i, ds: (i, 0)),
        scratch_shapes=[pltpu.VMEM((TOP_K, tm, d), F32), pltpu.SemaphoreType.DMA],
    )
    return pl.pallas_call(
        functools.partial(_combine_kernel, tm=tm, row0=row0, ctx_len=ctx_len, final=final),
        grid_spec=grid_spec,
        out_shape=jax.ShapeDtypeStruct((n_tok, d), F32),
        compiler_params=_cparams(("arbitrary",)),
        name="moe_combine",
    )(dest, ys, top_w, x, mod, final_gain.reshape(1, d))


FFT_R = 128
FFT_N = FFT_R * FFT_R
HY_TC = 128


def _hypre_kernel(z_ref, zp_ref, zn_ref, w_ref, b_ref, oc_ref, ol_ref, *, tm, ctx_len, l_tot):
    i = pl.program_id(1)
    x = z_ref[...]
    loc = lax.broadcasted_iota(jnp.int32, (tm, 1), 0)
    row = i * tm + loc
    prev = jnp.where(loc == 0, zp_ref[7:8, :], pltpu.roll(x, 1, 0))
    prev = jnp.where((row == 0) | (row == ctx_len), 0.0, prev)
    nxt = jnp.where(loc == tm - 1, zn_ref[0:1, :], pltpu.roll(x, tm - 1, 0))
    nxt = jnp.where((row == ctx_len - 1) | (row == l_tot - 1), 0.0, nxt)
    y = w_ref[0:1, :] * prev + w_ref[1:2, :] * x + w_ref[2:3, :] * nxt + b_ref[...]

    @pl.when(i == 0)
    def _():
        oc_ref[...] = y

    @pl.when(i > 0)
    def _():
        ol_ref[...] = y


def hyena_pre(z, conv_w, conv_b, ctx_len):
    l = z.shape[0]
    tm, tc = ctx_len, 512
    cb0 = RET_COLS // tc
    nr8 = l // 8
    return pl.pallas_call(
        functools.partial(_hypre_kernel, tm=tm, ctx_len=ctx_len, l_tot=l),
        grid=(HY_COLS // tc, l // tm),
        in_specs=[pl.BlockSpec((tm, tc), lambda c, i: (i, cb0 + c)),
                  pl.BlockSpec((8, tc), lambda c, i: (jnp.maximum(i * (tm // 8) - 1, 0), cb0 + c)),
                  pl.BlockSpec((8, tc), lambda c, i: (jnp.minimum((i + 1) * (tm // 8), nr8 - 1), cb0 + c)),
                  pl.BlockSpec((3, tc), lambda c, i: (0, c)),
                  pl.BlockSpec((1, tc), lambda c, i: (0, c))],
        out_specs=[pl.BlockSpec((tm, tc), lambda c, i: (0, c)),
                   pl.BlockSpec((tm, tc), lambda c, i: (jnp.maximum(i - 1, 0), c))],
        out_shape=[jax.ShapeDtypeStruct((ctx_len, HY_COLS), F32), jax.ShapeDtypeStruct((l - ctx_len, HY_COLS), F32)],
        compiler_params=_cparams(("arbitrary", "arbitrary")),
        name="hyena_pre",
    )(z, z, z, conv_w, conv_b.reshape(1, HY_COLS))


def _hyfilt_kernel(ft_ref, aux_ref, w1_ref, b1_ref, fr_ref, w2_ref, b2_ref, w3_ref, dl_ref, o_ref):
    fr = fr_ref[...]
    h = jnp.sin(fr * (_dot(ft_ref[...].astype(BF16), w1_ref[...].astype(BF16)) + b1_ref[...]))
    h = jnp.sin(fr * (_dot(h.astype(BF16), w2_ref[...].astype(BF16)) + b2_ref[...]))
    h = _dot(h.astype(BF16), w3_ref[...].astype(BF16))
    aux = aux_ref[...]
    t, m_f, m_b = aux[:, 0:1], aux[:, 1:2], aux[:, 2:3]
    window = jnp.exp(-t * dl_ref[...])
    for o in range(2):
        base = o * 2 * HY_W
        o_ref[:, o * HY_W:(o + 1) * HY_W] = (m_f * h[:, base:base + HY_W] + m_b * h[:, base + HY_W:base + 2 * HY_W]) * window


def hyena_filters(seq_len, w1, b1, freq, w2, b2, w3):
    L = seq_len
    r = np.arange(FFT_N)
    is_f, is_b = r < L, r > FFT_N - L
    pos = np.where(is_f, r, np.where(is_b, FFT_N - r, 0))
    t_lin = jnp.linspace(0.0, 1.0, L, dtype=F32)[:, None]
    w_ang = 2.0 * math.pi * jnp.arange(L, dtype=F32)[:, None] / L
    f = jnp.linspace(1e-4, HY_BANDS - 1, HY_BANDS, dtype=F32)[None, :]
    feats = jnp.concatenate([t_lin, jnp.cos(f * w_ang), -jnp.sin(f * w_ang)], axis=-1)[pos]
    ne, nw = feats.shape[1], w1.shape[1]
    feats = jnp.pad(feats, ((0, 0), (0, LANES - ne)))
    aux = jnp.zeros((FFT_N, LANES), F32).at[:, 0].set(t_lin[pos, 0]).at[:, 1].set(jnp.asarray(is_f, F32))
    aux = aux.at[:, 2].set(jnp.asarray(is_b, F32))
    w1p = jnp.pad(w1, ((0, LANES - ne), (0, LANES - nw)))
    w2p = jnp.pad(w2, ((0, LANES - nw), (0, LANES - nw)))
    w3p = jnp.pad(w3, ((0, LANES - nw), (0, 0)))
    pad1 = lambda a: jnp.pad(a, (0, LANES - nw)).reshape(1, LANES)
    deltas = jnp.abs(jnp.linspace(HY_MIN_DECAY, HY_MAX_DECAY, HY_W, dtype=F32)).reshape(1, HY_W)
    tm = 1024
    full = lambda shape: pl.BlockSpec(shape, lambda i: (0,) * len(shape))
    return pl.pallas_call(
        _hyfilt_kernel,
        grid=(FFT_N // tm,),
        in_specs=[pl.BlockSpec((tm, LANES), lambda i: (i, 0)), pl.BlockSpec((tm, LANES), lambda i: (i, 0)),
                  full((LANES, LANES)), full((1, LANES)), full((1, LANES)), full((LANES, LANES)), full((1, LANES)),
                  full((LANES, 4 * HY_W)), full((1, HY_W))],
        out_specs=pl.BlockSpec((tm, 2 * HY_W), lambda i: (i, 0)),
        out_shape=jax.ShapeDtypeStruct((FFT_N, 2 * HY_W), F32),
        compiler_params=_cparams(("arbitrary",)),
        name="hyena_filters",
    )(feats, aux, w1p, pad1(b1), pad1(freq), w2p, pad1(b2), w3p, deltas)


def _dft_consts():
    k = np.arange(FFT_R)
    ang = 2.0 * np.pi * np.outer(k, k) / FFT_R
    c, s = np.cos(ang), np.sin(ang)
    fwd_a = np.concatenate([c, -s], axis=0)
    fwd_b = np.block([[c, s], [-s, c]])
    inv_b = np.block([[c, -s], [s, c]])
    inv_a = np.concatenate([c, -s], axis=1) / FFT_N
    tw = 2.0 * np.pi * np.outer(k, k) / FFT_N
    as32 = lambda a: jnp.asarray(a, F32)
    return (as32(fwd_a), as32(fwd_b), as32(inv_b), as32(inv_a),
            as32(np.cos(tw))[:, :, None], as32(-np.sin(tw))[:, :, None])


def _store_grouped(o_ref, idx, res):
    for part in range(2):
        for g in range(FFT_R // 8):
            o_ref[g, idx, part] = res[part * FFT_R + g * 8:part * FFT_R + (g + 1) * 8, :]


def _fft_a_kernel(x_ref, f_ref, o_ref, *, n1, natural):
    f = f_ref[...].astype(BF16)
    for n2 in range(FFT_R):
        xs = x_ref[pl.ds(n2, n1, stride=FFT_R), :] if natural else x_ref[n2]
        _store_grouped(o_ref, n2, _dot(f, xs.astype(BF16)))


def fft_stage_a(x, n1, natural, c, col0=0):
    fwd_a = _dft_consts()[0][:, :n1]
    xspec = (pl.BlockSpec((n1 * FFT_R, HY_TC), lambda j: (0, col0 + j)) if natural
             else pl.BlockSpec((FFT_R, n1, HY_TC), lambda j: (0, 0, col0 + j)))
    out = pl.pallas_call(
        functools.partial(_fft_a_kernel, n1=n1, natural=natural),
        grid=(c // HY_TC,),
        in_specs=[xspec, pl.BlockSpec((2 * FFT_R, n1), lambda j: (0, 0))],
        out_specs=pl.BlockSpec((FFT_R // 8, FFT_R, 2, 8, HY_TC), lambda j: (0, 0, 0, 0, j)),
        out_shape=jax.ShapeDtypeStruct((FFT_R // 8, FFT_R, 2, 8, c), F32),
        compiler_params=_cparams(("arbitrary",)),
        name="fft_stage_a",
    )(x, fwd_a)
    return out.reshape(FFT_R // 8, FFT_R * 16, c)


def _load_pair(ref, j):
    return jnp.concatenate([ref[pl.ds(j, FFT_R, stride=16), :], ref[pl.ds(8 + j, FFT_R, stride=16), :]], axis=0)


def _twiddle(v, tr, ti):
    vr, vi = v[:FFT_R], v[FFT_R:]
    return jnp.concatenate([vr * tr - vi * ti, vr * ti + vi * tr], axis=0)


def _fft_b_spec_kernel(a_ref, twr_ref, twi_ref, fb_ref, o_ref):
    fb = fb_ref[...].astype(BF16)
    for kk in range(8):
        p = _twiddle(_load_pair(a_ref, kk), twr_ref[kk], twi_ref[kk])
        o_ref[kk] = _dot(fb, p.astype(BF16))


def _fft_b_conv_kernel(a_ref, h_ref, twr_ref, twi_ref, fb_ref, fib_ref, o_ref):
    fb = fb_ref[...].astype(BF16)
    fib = fib_ref[...].astype(BF16)
    for kk in range(8):
        tr, ti = twr_ref[kk], twi_ref[kk]
        x = _dot(fb, _twiddle(_load_pair(a_ref, kk), tr, ti).astype(BF16))
        h = h_ref[kk]
        y = _twiddle(x, h[:FFT_R], h[FFT_R:])
        z = _twiddle(_dot(fib, y.astype(BF16)), tr, -ti)
        for part in range(2):
            for g in range(FFT_R // 8):
                o_ref[g, kk, part] = z[part * FFT_R + g * 8:part * FFT_R + (g + 1) * 8, :]


def _bspecs(c):
    del c
    return [pl.BlockSpec((None, FFT_R * 16, HY_TC), lambda j, g: (g, 0, j))]


def fft_stage_b_spec(a):
    c = a.shape[-1]
    _, fwd_b, _, _, twr, twi = _dft_consts()
    tws = pl.BlockSpec((8, FFT_R, 1), lambda j, g: (g, 0, 0))
    return pl.pallas_call(
        _fft_b_spec_kernel,
        grid=(c // HY_TC, FFT_R // 8),
        in_specs=_bspecs(c) + [tws, tws, pl.BlockSpec((2 * FFT_R, 2 * FFT_R), lambda j, g: (0, 0))],
        out_specs=pl.BlockSpec((8, 2 * FFT_R, HY_TC), lambda j, g: (g, 0, j)),
        out_shape=jax.ShapeDtypeStruct((FFT_R, 2 * FFT_R, c), F32),
        compiler_params=_cparams(("arbitrary", "arbitrary")),
        name="fft_stage_b_spec",
    )(a, twr, twi, fwd_b)


def fft_stage_b_conv(a, hspec, h_col0):
    c = a.shape[-1]
    _, fwd_b, inv_b, _, twr, twi = _dft_consts()
    tws = pl.BlockSpec((8, FFT_R, 1), lambda j, g: (g, 0, 0))
    sq = pl.BlockSpec((2 * FFT_R, 2 * FFT_R), lambda j, g: (0, 0))
    out = pl.pallas_call(
        _fft_b_conv_kernel,
        grid=(c // HY_TC, FFT_R // 8),
        in_specs=_bspecs(c) + [pl.BlockSpec((8, 2 * FFT_R, HY_TC), lambda j, g: (g, 0, h_col0 + j)), tws, tws, sq, sq],
        out_specs=pl.BlockSpec((FFT_R // 8, 8, 2, 8, HY_TC), lambda j, g: (0, g, 0, 0, j)),
        out_shape=jax.ShapeDtypeStruct((FFT_R // 8, FFT_R, 2, 8, c), F32),
        compiler_params=_cparams(("arbitrary", "arbitrary")),
        name="fft_stage_b_conv",
    )(a, hspec, twr, twi, fwd_b, inv_b)
    return out.reshape(FFT_R // 8, FFT_R * 16, c)


def _fft_ainv_kernel(z_ref, u_ref, gate_ref, skip_ref, fi_ref, o_ref, *, n1, u_natural):
    g2 = pl.program_id(1)
    fi = fi_ref[...].astype(BF16)
    span = (n1 - 1) * FFT_R + 8
    base = pl.multiple_of(g2 * 8, 8)
    gate_rows = gate_ref.at[pl.ds(base, span)]
    u_rows = u_ref.at[pl.ds(base, span)] if u_natural else None
    for jj in range(8):
        y = _dot(fi, _load_pair(z_ref, jj).astype(BF16))
        u = u_rows[pl.ds(jj, n1, stride=FFT_R), :] if u_natural else u_ref[jj]
        o_ref[jj] = gate_rows[pl.ds(jj, n1, stride=FFT_R), :] * (y + skip_ref[...] * u)


def fft_stage_a_inv(z, u, gate, skip, gate_col, n1, u_natural):
    c = z.shape[-1]
    inv_a = _dft_consts()[3][:n1]
    ctiles = c // HY_TC
    uspec = (pl.BlockSpec((n1 * FFT_R, HY_TC), lambda j, g: (0, j)) if u_natural
             else pl.BlockSpec((8, n1, HY_TC), lambda j, g: (g, 0, j)))
    return pl.pallas_call(
        functools.partial(_fft_ainv_kernel, n1=n1, u_natural=u_natural),
        grid=(ctiles, FFT_R // 8),
        in_specs=[pl.BlockSpec((None, FFT_R * 16, HY_TC), lambda j, g: (g, 0, j)),
                  uspec,
                  pl.BlockSpec((n1 * FFT_R, HY_TC), lambda j, g: (0, gate_col * ctiles + j)),
                  pl.BlockSpec((1, HY_TC), lambda j, g: (0, j)),
                  pl.BlockSpec((n1, 2 * FFT_R), lambda j, g: (0, 0))],
        out_specs=pl.BlockSpec((8, n1, HY_TC), lambda j, g: (g, 0, j)),
        out_shape=jax.ShapeDtypeStruct((FFT_R, n1, c), F32),
        compiler_params=_cparams(("arbitrary", "arbitrary")),
        name="fft_stage_a_inv",
    )(z, u, gate, skip.reshape(1, c), inv_a)


def hyena_long(u3, filt, skip):
    n1 = u3.shape[0] // FFT_R
    ct = HY_W // HY_TC
    hspec = fft_stage_b_spec(fft_stage_a(filt, FFT_R, True, 2 * HY_W))
    a = fft_stage_a(u3, n1, True, HY_W)
    y1 = fft_stage_a_inv(fft_stage_b_conv(a, hspec, 0), u3, u3, skip[0], 1, n1, True)
    a = fft_stage_a(y1, n1, False, HY_W)
    return fft_stage_a_inv(fft_stage_b_conv(a, hspec, ct), y1, u3, skip[1], 2, n1, False)


def _unpermute(y):
    return jnp.swapaxes(y, 0, 1).reshape(-1, y.shape[-1])


def _rope_tables(ctx_len, n_rows):
    nf = HEAD_DIM // 4
    inv = ROPE_BASE ** (-jnp.arange(nf, dtype=F32) / nf)
    rows = jnp.repeat(jnp.arange(n_rows, dtype=F32), GRID_W)
    cols = (jnp.arange(n_rows * GRID_W) % GRID_W).astype(F32)
    ang = jnp.concatenate([rows[:, None] * inv, cols[:, None] * inv], axis=-1)
    ang = jnp.concatenate([jnp.zeros((ctx_len, HEAD_DIM // 2), F32), ang], axis=0)
    cos_t, sin_t = jnp.cos(ang), jnp.sin(ang)
    return jnp.concatenate([cos_t, cos_t], axis=-1), jnp.concatenate([-sin_t, sin_t], axis=-1)


def _moe(h2, x, mod_gate, final_gain, layer_p, row0, n_tok, ctx_len, final):
    router_w, router_b, w_gate, b_gate, w_up, b_up, w_down, b_down = layer_p
    top_idx_p, top_w_p = router(h2, router_w, router_b, row0, n_tok)
    top_idx = top_idx_p[:, :TOP_K]
    onehot = (top_idx[:, :, None] == jnp.arange(N_EXPERTS)[None, None, :]).astype(jnp.int32)
    per_tok = onehot.sum(axis=1)
    before = jnp.cumsum(per_tok, axis=0) - per_tok
    counts = per_tok.sum(axis=0)
    nblk = (counts + MOE_TM - 1) // MOE_TM
    blk_end = jnp.cumsum(nblk)
    pad_start = (blk_end - nblk) * MOE_TM
    dest = (pad_start[top_idx] + jnp.take_along_axis(before, top_idx, axis=1)).astype(jnp.int32)
    n_blocks = -(-(n_tok * TOP_K + N_EXPERTS * (MOE_TM - 1)) // MOE_TM)
    blk_e = jnp.minimum(jnp.searchsorted(blk_end, jnp.arange(n_blocks), side='right'), N_EXPERTS - 1)
    n_used = blk_end[-1:].astype(jnp.int32)
    tok = jnp.broadcast_to(jnp.arange(n_tok, dtype=jnp.int32)[:, None], dest.shape)
    slot_tok = jnp.zeros((n_blocks * MOE_TM,), jnp.int32).at[dest.reshape(-1)].set(tok.reshape(-1))
    ys = expert_ffn(h2, row0, slot_tok, blk_e.astype(jnp.int32), n_used, w_gate, b_gate, w_up, b_up, w_down, b_down)
    return moe_combine(ys, dest.reshape(-1), top_w_p, x, mod_gate, final_gain, row0, n_tok, ctx_len, final)


def kernel(x, c, ctx, c_ctx, norm_mix_g, norm_ffn_g, final_norm_g, ada_w, ada_b, w_in, w_out, ret_decay_logit,
           hy_conv_w, hy_conv_b, hy_filt_w1, hy_filt_b1, hy_filt_freq, hy_filt_w2, hy_filt_b2, hy_filt_w3, hy_skip,
           hg_lb_logits, router_w, router_b, moe_w_gate, moe_b_gate, moe_w_up, moe_b_up, moe_w_down, moe_b_down):
    bsz, seq, d = x.shape
    assert bsz == 1
    ctx_len = ctx.shape[1]
    l = ctx_len + seq
    cos2, sin2 = _rope_tables(ctx_len, seq // GRID_W)
    cvec = jnp.zeros((8, d), F32).at[0].set(c_ctx).at[1].set(c[0])
    mod = adaln(cvec, ada_w, ada_b)
    xs = jnp.concatenate([ctx[0], x[0]], axis=0)
    p_lb = jax.nn.softmax(hg_lb_logits.astype(F32), axis=0)
    lb_cum = jnp.cumsum(p_lb, axis=0)
    for layer in range(DEPTH):
        last = layer == DEPTH - 1
        m = mod[layer].reshape(8, N_MOD, d)
        mc, ml = m[0], m[1]
        mod_in = jnp.stack([mc[0], mc[1], ml[0], ml[1], mc[0], mc[0], mc[0], mc[0]])
        z = in_proj(xs, norm_mix_g[layer], mod_in, w_in[layer].astype(BF16), ctx_len, tm=768, tn=768)
        log_gamma = jax.nn.log_sigmoid(ret_decay_logit[layer].astype(F32))
        o_ret = retention_dir(z, log_gamma, cos2, sin2, None, ctx_len, reverse=False)
        o_ret = retention_dir(z, log_gamma, cos2, sin2, o_ret, ctx_len, reverse=True)
        lb = (lb_cum[layer] - p_lb[0]).reshape(2, HG_H, 1, HEAD_DIM)
        o_hg = hgrn2_dir(z, lb, None, ctx_len, reverse=False)
        o_hg = hgrn2_dir(z, lb, o_hg, ctx_len, reverse=True)
        filt_p = (hy_filt_w1[layer], hy_filt_b1[layer], hy_filt_freq[layer], hy_filt_w2[layer], hy_filt_b2[layer],
                  hy_filt_w3[layer])
        u_ctx, u_lat = hyena_pre(z, hy_conv_w[layer], hy_conv_b[layer], ctx_len)
        o_hy_l = _unpermute(hyena_long(u_lat, hyena_filters(seq, *filt_p), hy_skip[layer]))
        if last:
            o_hy_c = jnp.zeros((ctx_len, HY_W), F32)
        else:
            u_ctx = jnp.pad(u_ctx, ((0, seq - ctx_len), (0, 0)))
            o_hy_c = _unpermute(hyena_long(u_ctx, hyena_filters(ctx_len, *filt_p), hy_skip[layer]))[:ctx_len]
        o_hy = jnp.concatenate([o_hy_c, o_hy_l], axis=0)
        mod_out = jnp.stack([mc[2], ml[2], mc[3], mc[4], ml[3], ml[4], mc[0], mc[0]])
        xs, h2 = out_proj(o_ret, o_hy, o_hg, w_out[layer].astype(BF16), xs, norm_ffn_g[layer], mod_out, ctx_len,
                          tm=384)
        moe_p = (router_w[layer], router_b[layer], moe_w_gate[layer], moe_b_gate[layer], moe_w_up[layer],
                 moe_b_up[layer], moe_w_down[layer], moe_b_down[layer])
        mod_gate = jnp.stack([mc[5], ml[5], mc[0], mc[0], mc[0], mc[0], mc[0], mc[0]])
        if last:
            return _moe(h2, xs, mod_gate, final_norm_g, moe_p, ctx_len, seq, ctx_len, True)[None]
        xs = _moe(h2, xs, mod_gate, final_norm_g, moe_p, 0, l, ctx_len, False)
```

```python
import functools
import math

import jax
import jax.numpy as jnp
import numpy as np
from jax import lax
from jax.experimental import pallas as pl
from jax.experimental.pallas import tpu as pltpu

F32 = jnp.float32
BF16 = jnp.bfloat16

D_MODEL = 2048
DEPTH = 2
GRID_W = 64
HEAD_DIM = 128
RET_W = 3 * D_MODEL // 8
RET_H = RET_W // HEAD_DIM
HY_W = D_MODEL // 4
HG_W = D_MODEL - RET_W - HY_W
HG_H = HG_W // HEAD_DIM
RET_COLS = 4 * RET_W
HY_COLS = 3 * HY_W
HG_COLS = 5 * HG_W
IN_W = RET_COLS + HY_COLS + HG_COLS
ROPE_BASE = 10000.0
EPS = 1e-6
N_MOD = 6
HY_BANDS = 16
HY_MAX_DECAY = math.log(1e-2) / 0.3
HY_MIN_DECAY = math.log(1e-2) / 1.5
N_EXPERTS = 32
TOP_K = 4
SWIGLU_LIMIT = 7.0
SWIGLU_ALPHA = 1.702

LANES = 128
VMEM_LIMIT = 56 * 1024 * 1024

RET_CHUNK = 256
RET_HEADS_PER_STEP = 3
HG_CHUNK = 128
HG_SUB = 16
HG_HEADS_PER_STEP = 2
MOE_TM = 512
MOE_TF = 512


def _cparams(sem):
    return pltpu.CompilerParams(dimension_semantics=sem, vmem_limit_bytes=VMEM_LIMIT)


def _dot(a, b):
    return jnp.dot(a, b, preferred_element_type=F32)


def _dot_nt(a, b):
    return lax.dot_general(a, b, (((1,), (1,)), ((), ())), preferred_element_type=F32)


def _adaln_kernel(c_ref, w_ref, b_ref, o_ref):
    c = c_ref[...]
    cond = c * jax.nn.sigmoid(c)
    o_ref[0] = _dot(cond.astype(BF16), w_ref[0].astype(BF16)) + b_ref[0]


def adaln(cvec, ada_w, ada_b, tn=1536):
    depth, d, n = ada_w.shape
    return pl.pallas_call(
        _adaln_kernel,
        grid=(depth, n // tn),
        in_specs=[pl.BlockSpec((8, d), lambda l, j: (0, 0)),
                  pl.BlockSpec((1, d, tn), lambda l, j: (l, 0, j)),
                  pl.BlockSpec((1, 1, tn), lambda l, j: (l, 0, j))],
        out_specs=pl.BlockSpec((1, 8, tn), lambda l, j: (l, 0, j)),
        out_shape=jax.ShapeDtypeStruct((depth, 8, n), F32),
        compiler_params=_cparams(("arbitrary", "arbitrary")),
        name="adaln",
    )(cvec, ada_w, ada_b.reshape(depth, 1, n))


def _norm_mod(x, gain, mod_ref, row0, ctx_len, shift_row):
    y = x * lax.rsqrt(jnp.mean(x * x, axis=-1, keepdims=True) + EPS) * gain
    row = row0 + lax.broadcasted_iota(jnp.int32, (x.shape[0], 1), 0)
    is_ctx = row < ctx_len
    shift = jnp.where(is_ctx, mod_ref[shift_row:shift_row + 1, :], mod_ref[shift_row + 2:shift_row + 3, :])
    scale = jnp.where(is_ctx, mod_ref[shift_row + 1:shift_row + 2, :], mod_ref[shift_row + 3:shift_row + 4, :])
    return y * (1.0 + scale) + shift


def _win_kernel(x_ref, g_ref, mod_ref, w_ref, o_ref, h_scr, *, tm, ctx_len):
    i = pl.program_id(0)

    @pl.when(pl.program_id(1) == 0)
    def _():
        h_scr[...] = _norm_mod(x_ref[...], g_ref[...], mod_ref, i * tm, ctx_len, 0).astype(BF16)

    o_ref[...] = _dot(h_scr[...], w_ref[...])


def in_proj(x, gain, mod, w_bf16, ctx_len, tm, tn):
    l, d = x.shape
    n = w_bf16.shape[1]
    return pl.pallas_call(
        functools.partial(_win_kernel, tm=tm, ctx_len=ctx_len),
        grid=(l // tm, n // tn),
        in_specs=[pl.BlockSpec((tm, d), lambda i, j: (i, 0)),
                  pl.BlockSpec((1, d), lambda i, j: (0, 0)),
                  pl.BlockSpec((8, d), lambda i, j: (0, 0)),
                  pl.BlockSpec((d, tn), lambda i, j: (0, j))],
        out_specs=pl.BlockSpec((tm, tn), lambda i, j: (i, j)),
        out_shape=jax.ShapeDtypeStruct((l, n), F32),
        scratch_shapes=[pltpu.VMEM((tm, d), BF16)],
        compiler_params=_cparams(("arbitrary", "arbitrary")),
        name="in_proj",
    )(x, gain.reshape(1, d), mod, w_bf16)


def _chunk_of_step(i, n_ctx, n_tot, reverse):
    if not reverse:
        return i
    return jnp.where(i < n_ctx, n_ctx - 1 - i, n_tot - 1 - (i - n_ctx))


def _head_norm_gate(o, gate):
    return gate * (o * lax.rsqrt(jnp.mean(o * o, axis=-1, keepdims=True) + EPS))


def _ret_kernel(lg_ref, q_ref, k_ref, v_ref, cos_ref, sin_ref, *rest, reverse, final, chunk, heads):
    if final:
        of_ref, g_ref, o_ref, st_scr = rest
    else:
        o_ref, st_scr = rest
    c = chunk

    @pl.when(pl.program_id(1) == 0)
    def _():
        st_scr[...] = jnp.zeros_like(st_scr)

    cos2 = cos_ref[...]
    sin2 = sin_ref[...]

    def rope(x):
        return x * cos2 + pltpu.roll(x, HEAD_DIM // 2, 1) * sin2

    t = lax.broadcasted_iota(jnp.int32, (c, c), 0)
    s = lax.broadcasted_iota(jnp.int32, (c, c), 1)
    dist = (s - t) if reverse else (t - s)
    a = lax.broadcasted_iota(jnp.int32, (c, 1), 0)
    steps_q = ((c - a) if reverse else (a + 1)).astype(F32)
    steps_k = (a if reverse else (c - 1 - a)).astype(F32)
    for hh in range(heads):
        cols = slice(hh * HEAD_DIM, (hh + 1) * HEAD_DIM)
        lg = lg_ref[1 if reverse else 0, pl.program_id(0) * heads + hh]
        q = rope(q_ref[:, cols]).astype(BF16)
        k = rope(k_ref[:, cols]) * (HEAD_DIM ** -0.5)
        v = v_ref[:, cols]
        dec = jnp.where(dist >= 0, jnp.exp(lg * jnp.maximum(dist, 0).astype(F32)), 0.0)
        p = (_dot_nt(q, k.astype(BF16)) * dec).astype(BF16)
        st = st_scr[hh]
        o = _dot(p, v.astype(BF16)) + jnp.exp(lg * steps_q) * _dot_nt(q, st.astype(BF16))
        kd = (k * jnp.exp(lg * steps_k)).astype(BF16)
        st_scr[hh] = jnp.exp(lg * c) * st + _dot(v.T.astype(BF16), kd)
        if final:
            g = g_ref[:, cols]
            o_ref[:, cols] = _head_norm_gate(o + of_ref[:, cols], g * jax.nn.sigmoid(g)).astype(o_ref.dtype)
        else:
            o_ref[:, cols] = o


def retention_dir(z, log_gamma, cos2, sin2, o_fwd, ctx_len, reverse):
    l = z.shape[0]
    c, heads = RET_CHUNK, RET_HEADS_PER_STEP
    w = heads * HEAD_DIM
    n_tot, n_ctx = l // c, ctx_len // c
    hb = RET_W // w

    def zspec(col):
        return pl.BlockSpec((c, w), lambda h, i: (_chunk_of_step(i, n_ctx, n_tot, reverse), col + h))

    tspec = pl.BlockSpec((c, HEAD_DIM), lambda h, i: (_chunk_of_step(i, n_ctx, n_tot, reverse), 0))
    ospec = pl.BlockSpec((c, w), lambda h, i: (_chunk_of_step(i, n_ctx, n_tot, reverse), h))
    final = o_fwd is not None
    in_specs = [pl.BlockSpec(memory_space=pltpu.SMEM), zspec(0), zspec(hb), zspec(2 * hb), tspec, tspec]
    args = [log_gamma, z, z, z, cos2, sin2]
    if final:
        in_specs += [ospec, zspec(3 * hb)]
        args += [o_fwd, z]
    return pl.pallas_call(
        functools.partial(_ret_kernel, reverse=reverse, final=final, chunk=c, heads=heads),
        grid=(hb, n_tot),
        in_specs=in_specs,
        out_specs=ospec,
        out_shape=jax.ShapeDtypeStruct((l, RET_W), BF16 if final else F32),
        scratch_shapes=[pltpu.VMEM((heads, HEAD_DIM, HEAD_DIM), F32)],
        compiler_params=_cparams(("arbitrary", "arbitrary")),
        name="retention_bwd" if reverse else "retention_fwd",
    )(*args)


def _hg_kernel(q_ref, fz_ref, v_ref, lb_ref, tri_ref, sel_ref, *rest, reverse, final, chunk, sub, heads):
    if final:
        of_ref, g_ref, o_ref, st_scr, cum_scr, k_scr, p_scr = rest
    else:
        o_ref, st_scr, cum_scr, k_scr, p_scr = rest
    c = chunk

    @pl.when(pl.program_id(1) == 0)
    def _():
        st_scr[...] = jnp.zeros_like(st_scr)

    row = lax.broadcasted_iota(jnp.int32, (c, 1), 0)
    t = lax.broadcasted_iota(jnp.int32, (c, c), 0)
    s = lax.broadcasted_iota(jnp.int32, (c, c), 1)

    for hh in range(heads):
        cols = slice(hh * HEAD_DIM, (hh + 1) * HEAD_DIM)
        cum_h, k_h, p_h = cum_scr.at[hh], k_scr.at[hh], p_scr.at[hh]
        qr = q_ref[:, cols]
        q = qr * jax.nn.sigmoid(qr)
        lb = lb_ref[0][:, cols]
        sg = jax.nn.sigmoid(fz_ref[:, cols])
        logf = jnp.log(lb + (1.0 - lb) * sg)
        kk = (1.0 - lb) * (1.0 - sg)
        v32 = v_ref[:, cols]
        v = v32.astype(BF16)
        cum = jnp.dot(tri_ref[...], logf, precision=lax.Precision.HIGHEST, preferred_element_type=F32)
        cum_h[...] = cum
        k_h[...] = kk

        a_mat = jnp.zeros((c, c), F32)
        m = c // 2
        while m >= sub:
            parts = []
            for p in range(c // (2 * m)):
                r = p * 2 * m + (m if reverse else m - 1)
                parts.append(jnp.broadcast_to(cum_h[pl.ds(r, 1), :], (2 * m, HEAD_DIM)))
            dq = cum - (jnp.concatenate(parts, axis=0) if len(parts) > 1 else parts[0])
            second_half = ((row >> (m.bit_length() - 1)) & 1) == 1
            is_q = jnp.logical_not(second_half) if reverse else second_half
            eq = jnp.where(is_q, jnp.exp(jnp.minimum(dq, 0.0)), 0.0)
            ek = jnp.where(is_q, 0.0, jnp.exp(jnp.minimum(-dq, 0.0)))
            a_l = _dot_nt((q * eq).astype(BF16), (kk * ek).astype(BF16))
            a_mat = a_mat + jnp.where((t >> m.bit_length()) == (s >> m.bit_length()), a_l, 0.0)
            m //= 2

        for si in range(c):
            r0 = (si // sub) * sub
            j = si % sub
            rows = r0 + lax.broadcasted_iota(jnp.int32, (sub, 1), 0)
            keep = (rows <= si) if reverse else (rows >= si)
            e = jnp.exp(jnp.minimum(cum[r0:r0 + sub, :] - cum_h[pl.ds(si, 1), :], 0.0))
            pv = jnp.where(keep, q[r0:r0 + sub, :] * k_h[pl.ds(si, 1), :] * e, 0.0)
            p_h[r0:r0 + sub, j * HEAD_DIM:(j + 1) * HEAD_DIM] = pv.astype(BF16)
        a_diag = _dot(p_h[...], sel_ref[...])
        pieces = []
        for b in range(c // sub):
            blk = a_diag[b * sub:(b + 1) * sub, :]
            pieces.append(blk if b == 0 else pltpu.roll(blk, b * sub, 1))
        a_mat = a_mat + jnp.concatenate(pieces, axis=0)

        st = st_scr[hh]
        o = _dot(a_mat.astype(BF16), v) + _dot_nt((q * jnp.exp(cum)).astype(BF16), st.astype(BF16))
        last = cum_h[pl.ds(0 if reverse else c - 1, 1), :]
        kd = (kk * jnp.exp(last - cum)).astype(BF16)
        st_scr[hh] = st * jnp.exp(last) + _dot(v32.T.astype(BF16), kd)
        if final:
            o_ref[:, cols] = _head_norm_gate(o + of_ref[:, cols], jax.nn.sigmoid(g_ref[:, cols])).astype(o_ref.dtype)
        else:
            o_ref[:, cols] = o


def hgrn2_dir(z, lb, o_fwd, ctx_len, reverse):
    l = z.shape[0]
    c, sub, heads = HG_CHUNK, HG_SUB, HG_HEADS_PER_STEP
    w = heads * HEAD_DIM
    n_tot, n_ctx = l // c, ctx_len // c
    col0 = (RET_COLS + HY_COLS) // w
    hb = HG_W // w
    final = o_fwd is not None
    d = 1 if reverse else 0

    def zspec(col):
        return pl.BlockSpec((c, w), lambda h, i: (_chunk_of_step(i, n_ctx, n_tot, reverse), col0 + col + h))

    ospec = pl.BlockSpec((c, w), lambda h, i: (_chunk_of_step(i, n_ctx, n_tot, reverse), h))
    r = np.arange(c)
    tri = (r[None, :] >= r[:, None]) if reverse else (r[None, :] <= r[:, None])
    sel = (np.arange(sub * HEAD_DIM)[:, None] // HEAD_DIM) == np.arange(HEAD_DIM)[None, :]
    in_specs = [zspec(0), zspec((1 + d) * hb), zspec(3 * hb),
                pl.BlockSpec((1, 1, w), lambda h, i: (d * hb + h, 0, 0)),
                pl.BlockSpec((c, c), lambda h, i: (0, 0)),
                pl.BlockSpec((sub * HEAD_DIM, HEAD_DIM), lambda h, i: (0, 0))]
    args = [z, z, z, lb.reshape(2 * hb, 1, w), jnp.asarray(tri, F32), jnp.asarray(sel, BF16)]
    if final:
        in_specs += [ospec, zspec(4 * hb)]
        args += [o_fwd, z]
    return pl.pallas_call(
        functools.partial(_hg_kernel, reverse=reverse, final=final, chunk=c, sub=sub, heads=heads),
        grid=(hb, n_tot),
        in_specs=in_specs,
        out_specs=ospec,
        out_shape=jax.ShapeDtypeStruct((l, HG_W), BF16 if final else F32),
        scratch_shapes=[pltpu.VMEM((heads, HEAD_DIM, HEAD_DIM), F32), pltpu.VMEM((heads, c, HEAD_DIM), F32),
                        pltpu.VMEM((heads, c, HEAD_DIM), F32), pltpu.VMEM((heads, c, sub * HEAD_DIM), BF16)],
        compiler_params=_cparams(("arbitrary", "arbitrary")),
        name="hgrn2_bwd" if reverse else "hgrn2_fwd",
    )(*args)


def _wout_kernel(oret_ref, ohy_ref, ohg_ref, w_ref, x_ref, g_ref, mod_ref, xo_ref, h_ref, *, tm, ctx_len):
    i = pl.program_id(0)
    y = (_dot(oret_ref[...], w_ref[0:RET_W, :]) + _dot(ohy_ref[...].astype(BF16), w_ref[RET_W:RET_W + HY_W, :])
         + _dot(ohg_ref[...], w_ref[RET_W + HY_W:, :]))
    row = i * tm + lax.broadcasted_iota(jnp.int32, (tm, 1), 0)
    gate = jnp.where(row < ctx_len, mod_ref[0:1, :], mod_ref[1:2, :])
    x = x_ref[...] + gate * y
    xo_ref[...] = x
    h_ref[...] = _norm_mod(x, g_ref[...], mod_ref, i * tm, ctx_len, 2)


def out_proj(o_ret, o_hy, o_hg, w_bf16, x, gain, mod, ctx_len, tm):
    l, d = x.shape
    return pl.pallas_call(
        functools.partial(_wout_kernel, tm=tm, ctx_len=ctx_len),
        grid=(l // tm,),
        in_specs=[pl.BlockSpec((tm, RET_W), lambda i: (i, 0)),
                  pl.BlockSpec((tm, HY_W), lambda i: (i, 0)),
                  pl.BlockSpec((tm, HG_W), lambda i: (i, 0)),
                  pl.BlockSpec((d, d), lambda i: (0, 0)),
                  pl.BlockSpec((tm, d), lambda i: (i, 0)),
                  pl.BlockSpec((1, d), lambda i: (0, 0)),
                  pl.BlockSpec((8, d), lambda i: (0, 0))],
        out_specs=[pl.BlockSpec((tm, d), lambda i: (i, 0)), pl.BlockSpec((tm, d), lambda i: (i, 0))],
        out_shape=[jax.ShapeDtypeStruct((l, d), F32), jax.ShapeDtypeStruct((l, d), F32)],
        compiler_params=_cparams(("arbitrary",)),
        name="out_proj",
    )(o_ret, o_hy, o_hg, w_bf16, x, gain.reshape(1, d), mod)


def _router_kernel(h_ref, w_ref, b_ref, idx_ref, wt_ref):
    logits = _dot(h_ref[...].astype(BF16), w_ref[...]) + b_ref[...]
    lane = lax.broadcasted_iota(jnp.int32, logits.shape, 1).astype(F32)
    vals, idxs = [], []
    x = logits
    for _ in range(TOP_K):
        mx = jnp.max(x, axis=-1, keepdims=True)
        ix = jnp.min(jnp.where(x == mx, lane, float(LANES)), axis=-1, keepdims=True)
        vals.append(mx)
        idxs.append(ix)
        x = jnp.where(lane == ix, -jnp.inf, x)
    es = [jnp.exp(vv - vals[0]) for vv in vals]
    tot = es[0] + es[1] + es[2] + es[3]
    idx_out = jnp.zeros(logits.shape, F32)
    wt_out = jnp.zeros(logits.shape, F32)
    for r in range(TOP_K):
        idx_out = jnp.where(lane == float(r), idxs[r], idx_out)
        wt_out = jnp.where(lane == float(r), es[r] / tot, wt_out)
    idx_ref[...] = idx_out.astype(jnp.int32)
    wt_ref[...] = wt_out


def router(h2, router_w, router_b, row0, n_tok, tm=256):
    d = h2.shape[1]
    w = jnp.zeros((d, LANES), BF16).at[:, :N_EXPERTS].set(router_w.astype(BF16))
    b = jnp.full((1, LANES), -jnp.inf, F32).at[0, :N_EXPERTS].set(router_b)
    return pl.pallas_call(
        _router_kernel,
        grid=(n_tok // tm,),
        in_specs=[pl.BlockSpec((tm, d), lambda i: (i + row0 // tm, 0)),
                  pl.BlockSpec((d, LANES), lambda i: (0, 0)),
                  pl.BlockSpec((1, LANES), lambda i: (0, 0))],
        out_specs=[pl.BlockSpec((tm, LANES), lambda i: (i, 0)), pl.BlockSpec((tm, LANES), lambda i: (i, 0))],
        out_shape=[jax.ShapeDtypeStruct((n_tok, LANES), jnp.int32), jax.ShapeDtypeStruct((n_tok, LANES), F32)],
        compiler_params=_cparams(("arbitrary",)),
        name="router",
    )(h2, w, b)


def _row_copy(src_hbm, src_row, dst, dst_row, sem):
    return pltpu.make_async_copy(src_hbm.at[pl.ds(src_row, 1)], dst.at[pl.ds(dst_row, 1)], sem)


def _expert_kernel(blk_e_ref, nused_ref, valid_ref, tok_ref, h_hbm, wg_ref, bg_ref, wu_ref, bu_ref, wd_ref, bd_ref,
                   o_ref, xf_scr, xb_scr, sem, *, row0):
    b = pl.program_id(0)
    f = pl.program_id(1)
    nused = nused_ref[0]

    def gather(blk, wait):
        dst = xf_scr.at[blk % 2]

        def body(r, carry):
            cp = _row_copy(h_hbm, row0 + tok_ref[blk * MOE_TM + r], dst, r, sem.at[blk % 2])
            if wait:
                cp.wait()
            else:
                cp.start()
            return carry

        lax.fori_loop(0, valid_ref[blk], body, 0)

    @pl.when(b < nused)
    def _():
        @pl.when(f == 0)
        def _():
            @pl.when(b == 0)
            def _():
                xf_scr[...] = jnp.zeros_like(xf_scr)
                gather(0, False)

            gather(b, True)
            xb_scr[...] = xf_scr[b % 2].astype(BF16)

        @pl.when((f == 1) & (b + 1 < nused))
        def _():
            gather(b + 1, False)

        x = xb_scr[...]
        gate = jnp.minimum(_dot(x, wg_ref[0].astype(BF16)) + bg_ref[0], SWIGLU_LIMIT)
        up = jnp.clip(_dot(x, wu_ref[0].astype(BF16)) + bu_ref[0], -SWIGLU_LIMIT, SWIGLU_LIMIT)
        act = gate * jax.nn.sigmoid(SWIGLU_ALPHA * gate) * (up + 1.0)
        part = _dot(act.astype(BF16), wd_ref[0].astype(BF16))

        @pl.when(f == 0)
        def _():
            o_ref[...] = part + bd_ref[0]

        @pl.when(f != 0)
        def _():
            o_ref[...] += part

    @pl.when((b >= nused) & (f == 0))
    def _():
        o_ref[...] = jnp.zeros_like(o_ref)


def expert_ffn(h2, row0, slot_tok, blk_e, n_used, blk_valid, layer, w_gate, b_gate, w_up, b_up, w_down, b_down):
    d = h2.shape[1]
    n_rows = slot_tok.shape[0]
    depth, e, _, dff = w_gate.shape
    nb, nf = n_rows // MOE_TM, dff // MOE_TF
    assert nf >= 2

    def blk(b, nu):
        return jnp.minimum(b, nu[0] - 1)

    def fidx(b, f, nu):
        return jnp.where(b < nu[0], f, nf - 1)

    grid_spec = pltpu.PrefetchScalarGridSpec(
        num_scalar_prefetch=4,
        grid=(nb, nf),
        in_specs=[pl.BlockSpec(memory_space=pl.ANY),
                  pl.BlockSpec((None, 1, d, MOE_TF), lambda b, f, be, nu, vl, tk: (layer, be[blk(b, nu)], 0, fidx(b, f, nu))),
                  pl.BlockSpec((None, 1, 1, MOE_TF), lambda b, f, be, nu, vl, tk: (layer, be[blk(b, nu)], 0, fidx(b, f, nu))),
                  pl.BlockSpec((None, 1, d, MOE_TF), lambda b, f, be, nu, vl, tk: (layer, be[blk(b, nu)], 0, fidx(b, f, nu))),
                  pl.BlockSpec((None, 1, 1, MOE_TF), lambda b, f, be, nu, vl, tk: (layer, be[blk(b, nu)], 0, fidx(b, f, nu))),
                  pl.BlockSpec((None, 1, MOE_TF, d), lambda b, f, be, nu, vl, tk: (layer, be[blk(b, nu)], fidx(b, f, nu), 0)),
                  pl.BlockSpec((None, 1, 1, d), lambda b, f, be, nu, vl, tk: (layer, be[blk(b, nu)], 0, 0))],
        out_specs=pl.BlockSpec((MOE_TM, d), lambda b, f, be, nu, vl, tk: (b, 0)),
        scratch_shapes=[pltpu.VMEM((2, MOE_TM, d), F32), pltpu.VMEM((MOE_TM, d), BF16),
                        pltpu.SemaphoreType.DMA((2,))],
    )
    return pl.pallas_call(
        functools.partial(_expert_kernel, row0=row0),
        grid_spec=grid_spec,
        out_shape=jax.ShapeDtypeStruct((n_rows, d), F32),
        compiler_params=_cparams(("arbitrary", "arbitrary")),
        name="expert_ffn",
    )(blk_e, n_used, blk_valid, slot_tok, h2, w_gate, b_gate.reshape(depth, e, 1, dff), w_up,
      b_up.reshape(depth, e, 1, dff), w_down, b_down.reshape(depth, e, 1, d))


def _combine_kernel(dest_ref, y_hbm, wt_ref, x_ref, mod_ref, g_ref, o_ref, y_scr, sem, *, tm, row0, ctx_len, final):
    i = pl.program_id(0)

    def gather(tile, wait):
        def body(r, carry):
            for k in range(TOP_K):
                cp = _row_copy(y_hbm, dest_ref[(tile * tm + r) * TOP_K + k], y_scr.at[tile % 2, k], r, sem.at[tile % 2])
                if wait:
                    cp.wait()
                else:
                    cp.start()
            return carry

        lax.fori_loop(0, tm, body, 0, unroll=4)

    @pl.when(i == 0)
    def _():
        gather(0, False)

    @pl.when(i + 1 < pl.num_programs(0))
    def _():
        gather(i + 1, False)

    gather(i, True)
    wt = wt_ref[...]
    ys = y_scr.at[i % 2]
    acc = ys[0] * wt[:, 0:1]
    for k in range(1, TOP_K):
        acc = acc + ys[k] * wt[:, k:k + 1]
    row = row0 + i * tm + lax.broadcasted_iota(jnp.int32, (tm, 1), 0)
    x = x_ref[...] + jnp.where(row < ctx_len, mod_ref[0:1, :], mod_ref[1:2, :]) * acc
    if final:
        x = x * lax.rsqrt(jnp.mean(x * x, axis=-1, keepdims=True) + EPS) * g_ref[...]
    o_ref[...] = x


def moe_combine(ys, dest, top_w, x, mod, final_gain, row0, n_tok, ctx_len, final, tm=128):
    d = x.shape[1]
    grid_spec = pltpu.PrefetchScalarGridSpec(
        num_scalar_prefetch=1,
        grid=(n_tok // tm,),
        in_specs=[pl.BlockSpec(memory_space=pl.ANY),
                  pl.BlockSpec((tm, LANES), lambda i, ds: (i, 0)),
                  pl.BlockSpec((tm, d), lambda i, ds: (i + row0 // tm, 0)),
                  pl.BlockSpec((8, d), lambda i, ds: (0, 0)),
                  pl.BlockSpec((1, d), lambda i, ds: (0, 0))],
        out_specs=pl.BlockSpec((tm, d), lambda i, ds: (i, 0)),
        scratch_shapes=[pltpu.VMEM((2, TOP_K, tm, d), F32), pltpu.SemaphoreType.DMA((2,))],
    )
    return pl.pallas_call(
        functools.partial(_combine_kernel, tm=tm, row0=row0, ctx_len=ctx_len, final=final),
        grid_spec=grid_spec,
        out_shape=jax.ShapeDtypeStruct((n_tok, d), F32),
        compiler_params=_cparams(("arbitrary",)),
        name="moe_combine",
    )(dest, ys, top_w, x, mod, final_gain.reshape(1, d))


FFT_R = 128
FFT_N = FFT_R * FFT_R
HY_TC = 128


def _hypre_kernel(z_ref, zp_ref, zn_ref, w_ref, b_ref, oc_ref, ol_ref, *, tm, ctx_len, l_tot):
    i = pl.program_id(1)
    x = z_ref[...]
    loc = lax.broadcasted_iota(jnp.int32, (tm, 1), 0)
    row = i * tm + loc
    prev = jnp.where(loc == 0, zp_ref[7:8, :], pltpu.roll(x, 1, 0))
    prev = jnp.where((row == 0) | (row == ctx_len), 0.0, prev)
    nxt = jnp.where(loc == tm - 1, zn_ref[0:1, :], pltpu.roll(x, tm - 1, 0))
    nxt = jnp.where((row == ctx_len - 1) | (row == l_tot - 1), 0.0, nxt)
    y = w_ref[0:1, :] * prev + w_ref[1:2, :] * x + w_ref[2:3, :] * nxt + b_ref[...]

    @pl.when(i == 0)
    def _():
        oc_ref[...] = y

    @pl.when(i > 0)
    def _():
        ol_ref[...] = y


def hyena_pre(z, conv_w, conv_b, ctx_len):
    l = z.shape[0]
    tm, tc = ctx_len, 512
    cb0 = RET_COLS // tc
    nr8 = l // 8
    return pl.pallas_call(
        functools.partial(_hypre_kernel, tm=tm, ctx_len=ctx_len, l_tot=l),
        grid=(HY_COLS // tc, l // tm),
        in_specs=[pl.BlockSpec((tm, tc), lambda c, i: (i, cb0 + c)),
                  pl.BlockSpec((8, tc), lambda c, i: (jnp.maximum(i * (tm // 8) - 1, 0), cb0 + c)),
                  pl.BlockSpec((8, tc), lambda c, i: (jnp.minimum((i + 1) * (tm // 8), nr8 - 1), cb0 + c)),
                  pl.BlockSpec((3, tc), lambda c, i: (0, c)),
                  pl.BlockSpec((1, tc), lambda c, i: (0, c))],
        out_specs=[pl.BlockSpec((tm, tc), lambda c, i: (0, c)),
                   pl.BlockSpec((tm, tc), lambda c, i: (jnp.maximum(i - 1, 0), c))],
        out_shape=[jax.ShapeDtypeStruct((ctx_len, HY_COLS), F32), jax.ShapeDtypeStruct((l - ctx_len, HY_COLS), F32)],
        compiler_params=_cparams(("arbitrary", "arbitrary")),
        name="hyena_pre",
    )(z, z, z, conv_w, conv_b.reshape(1, HY_COLS))


def _hyfilt_kernel(ft_ref, aux_ref, w1_ref, b1_ref, fr_ref, w2_ref, b2_ref, w3_ref, dl_ref, o_ref):
    fr = fr_ref[...]
    h = jnp.sin(fr * (_dot(ft_ref[...].astype(BF16), w1_ref[...].astype(BF16)) + b1_ref[...]))
    h = jnp.sin(fr * (_dot(h.astype(BF16), w2_ref[...].astype(BF16)) + b2_ref[...]))
    h = _dot(h.astype(BF16), w3_ref[...].astype(BF16))
    aux = aux_ref[...]
    t, m_f, m_b = aux[:, 0:1], aux[:, 1:2], aux[:, 2:3]
    window = jnp.exp(-t * dl_ref[...])
    for o in range(2):
        base = o * 2 * HY_W
        o_ref[:, o * HY_W:(o + 1) * HY_W] = (m_f * h[:, base:base + HY_W] + m_b * h[:, base + HY_W:base + 2 * HY_W]) * window


def hyena_filters(seq_len, w1, b1, freq, w2, b2, w3):
    L = seq_len
    r = np.arange(FFT_N)
    is_f, is_b = r < L, r > FFT_N - L
    pos = np.where(is_f, r, np.where(is_b, FFT_N - r, 0))
    t_lin = jnp.linspace(0.0, 1.0, L, dtype=F32)[:, None]
    w_ang = 2.0 * math.pi * jnp.arange(L, dtype=F32)[:, None] / L
    f = jnp.linspace(1e-4, HY_BANDS - 1, HY_BANDS, dtype=F32)[None, :]
    feats = jnp.concatenate([t_lin, jnp.cos(f * w_ang), -jnp.sin(f * w_ang)], axis=-1)[pos]
    ne, nw = feats.shape[1], w1.shape[1]
    feats = jnp.pad(feats, ((0, 0), (0, LANES - ne)))
    aux = jnp.zeros((FFT_N, LANES), F32).at[:, 0].set(t_lin[pos, 0]).at[:, 1].set(jnp.asarray(is_f, F32))
    aux = aux.at[:, 2].set(jnp.asarray(is_b, F32))
    w1p = jnp.pad(w1, ((0, LANES - ne), (0, LANES - nw)))
    w2p = jnp.pad(w2, ((0, LANES - nw), (0, LANES - nw)))
    w3p = jnp.pad(w3, ((0, LANES - nw), (0, 0)))
    pad1 = lambda a: jnp.pad(a, (0, LANES - nw)).reshape(1, LANES)
    deltas = jnp.abs(jnp.linspace(HY_MIN_DECAY, HY_MAX_DECAY, HY_W, dtype=F32)).reshape(1, HY_W)
    tm = 1024
    full = lambda shape: pl.BlockSpec(shape, lambda i: (0,) * len(shape))
    return pl.pallas_call(
        _hyfilt_kernel,
        grid=(FFT_N // tm,),
        in_specs=[pl.BlockSpec((tm, LANES), lambda i: (i, 0)), pl.BlockSpec((tm, LANES), lambda i: (i, 0)),
                  full((LANES, LANES)), full((1, LANES)), full((1, LANES)), full((LANES, LANES)), full((1, LANES)),
                  full((LANES, 4 * HY_W)), full((1, HY_W))],
        out_specs=pl.BlockSpec((tm, 2 * HY_W), lambda i: (i, 0)),
        out_shape=jax.ShapeDtypeStruct((FFT_N, 2 * HY_W), F32),
        compiler_params=_cparams(("arbitrary",)),
        name="hyena_filters",
    )(feats, aux, w1p, pad1(b1), pad1(freq), w2p, pad1(b2), w3p, deltas)


def _dft_consts():
    k = np.arange(FFT_R)
    ang = 2.0 * np.pi * np.outer(k, k) / FFT_R
    c, s = np.cos(ang), np.sin(ang)
    fwd_a = np.concatenate([c, -s], axis=0)
    fwd_b = np.block([[c, s], [-s, c]])
    inv_b = np.block([[c, -s], [s, c]])
    inv_a = np.concatenate([c, -s], axis=1) / FFT_N
    tw = 2.0 * np.pi * np.outer(k, k) / FFT_N
    as32 = lambda a: jnp.asarray(a, F32)
    return (as32(fwd_a), as32(fwd_b), as32(inv_b), as32(inv_a),
            as32(np.cos(tw))[:, :, None], as32(-np.sin(tw))[:, :, None])


def _store_grouped(o_ref, idx, res):
    for part in range(2):
        for g in range(FFT_R // 8):
            o_ref[g, idx, part] = res[part * FFT_R + g * 8:part * FFT_R + (g + 1) * 8, :]


def _fft_a_kernel(x_ref, f_ref, o_ref, *, n1, natural):
    f = f_ref[...].astype(BF16)
    for n2 in range(FFT_R):
        xs = x_ref[pl.ds(n2, n1, stride=FFT_R), :] if natural else x_ref[n2]
        _store_grouped(o_ref, n2, _dot(f, xs.astype(BF16)))


def fft_stage_a(x, n1, natural, c, col0=0):
    fwd_a = _dft_consts()[0][:, :n1]
    xspec = (pl.BlockSpec((n1 * FFT_R, HY_TC), lambda j: (0, col0 + j)) if natural
             else pl.BlockSpec((FFT_R, n1, HY_TC), lambda j: (0, 0, col0 + j)))
    out = pl.pallas_call(
        functools.partial(_fft_a_kernel, n1=n1, natural=natural),
        grid=(c // HY_TC,),
        in_specs=[xspec, pl.BlockSpec((2 * FFT_R, n1), lambda j: (0, 0))],
        out_specs=pl.BlockSpec((FFT_R // 8, FFT_R, 2, 8, HY_TC), lambda j: (0, 0, 0, 0, j)),
        out_shape=jax.ShapeDtypeStruct((FFT_R // 8, FFT_R, 2, 8, c), F32),
        compiler_params=_cparams(("arbitrary",)),
        name="fft_stage_a",
    )(x, fwd_a)
    return out.reshape(FFT_R // 8, FFT_R * 16, c)


def _load_pair(ref, j):
    return jnp.concatenate([ref[pl.ds(j, FFT_R, stride=16), :], ref[pl.ds(8 + j, FFT_R, stride=16), :]], axis=0)


def _twiddle(v, tr, ti):
    vr, vi = v[:FFT_R], v[FFT_R:]
    return jnp.concatenate([vr * tr - vi * ti, vr * ti + vi * tr], axis=0)


def _fft_b_spec_kernel(a_ref, twr_ref, twi_ref, fb_ref, o_ref):
    fb = fb_ref[...].astype(BF16)
    for kk in range(8):
        p = _twiddle(_load_pair(a_ref, kk), twr_ref[kk], twi_ref[kk])
        o_ref[kk] = _dot(fb, p.astype(BF16))


def _fft_b_conv_kernel(a_ref, h_ref, twr_ref, twi_ref, fb_ref, fib_ref, o_ref):
    fb = fb_ref[...].astype(BF16)
    fib = fib_ref[...].astype(BF16)
    for kk in range(8):
        tr, ti = twr_ref[kk], twi_ref[kk]
        x = _dot(fb, _twiddle(_load_pair(a_ref, kk), tr, ti).astype(BF16))
        h = h_ref[kk]
        y = _twiddle(x, h[:FFT_R], h[FFT_R:])
        z = _twiddle(_dot(fib, y.astype(BF16)), tr, -ti)
        for part in range(2):
            for g in range(FFT_R // 8):
                o_ref[g, kk, part] = z[part * FFT_R + g * 8:part * FFT_R + (g + 1) * 8, :]


def _bspecs(c):
    del c
    return [pl.BlockSpec((None, FFT_R * 16, HY_TC), lambda j, g: (g, 0, j))]


def fft_stage_b_spec(a):
    c = a.shape[-1]
    _, fwd_b, _, _, twr, twi = _dft_consts()
    tws = pl.BlockSpec((8, FFT_R, 1), lambda j, g: (g, 0, 0))
    return pl.pallas_call(
        _fft_b_spec_kernel,
        grid=(c // HY_TC, FFT_R // 8),
        in_specs=_bspecs(c) + [tws, tws, pl.BlockSpec((2 * FFT_R, 2 * FFT_R), lambda j, g: (0, 0))],
        out_specs=pl.BlockSpec((8, 2 * FFT_R, HY_TC), lambda j, g: (g, 0, j)),
        out_shape=jax.ShapeDtypeStruct((FFT_R, 2 * FFT_R, c), F32),
        compiler_params=_cparams(("arbitrary", "arbitrary")),
        name="fft_stage_b_spec",
    )(a, twr, twi, fwd_b)


def fft_stage_b_conv(a, hspec, h_col0):
    c = a.shape[-1]
    _, fwd_b, inv_b, _, twr, twi = _dft_consts()
    tws = pl.BlockSpec((8, FFT_R, 1), lambda j, g: (g, 0, 0))
    sq = pl.BlockSpec((2 * FFT_R, 2 * FFT_R), lambda j, g: (0, 0))
    out = pl.pallas_call(
        _fft_b_conv_kernel,
        grid=(c // HY_TC, FFT_R // 8),
        in_specs=_bspecs(c) + [pl.BlockSpec((8, 2 * FFT_R, HY_TC), lambda j, g: (g, 0, h_col0 + j)), tws, tws, sq, sq],
        out_specs=pl.BlockSpec((FFT_R // 8, 8, 2, 8, HY_TC), lambda j, g: (0, g, 0, 0, j)),
        out_shape=jax.ShapeDtypeStruct((FFT_R // 8, FFT_R, 2, 8, c), F32),
        compiler_params=_cparams(("arbitrary", "arbitrary")),
        name="fft_stage_b_conv",
    )(a, hspec, twr, twi, fwd_b, inv_b)
    return out.reshape(FFT_R // 8, FFT_R * 16, c)


def _fft_ainv_kernel(z_ref, u_ref, gate_ref, skip_ref, fi_ref, o_ref, *, n1, u_natural):
    g2 = pl.program_id(1)
    fi = fi_ref[...].astype(BF16)
    span = (n1 - 1) * FFT_R + 8
    base = pl.multiple_of(g2 * 8, 8)
    gate_rows = gate_ref.at[pl.ds(base, span)]
    u_rows = u_ref.at[pl.ds(base, span)] if u_natural else None
    for jj in range(8):
        y = _dot(fi, _load_pair(z_ref, jj).astype(BF16))
        u = u_rows[pl.ds(jj, n1, stride=FFT_R), :] if u_natural else u_ref[jj]
        o_ref[jj] = gate_rows[pl.ds(jj, n1, stride=FFT_R), :] * (y + skip_ref[...] * u)


def fft_stage_a_inv(z, u, gate, skip, gate_col, n1, u_natural):
    c = z.shape[-1]
    inv_a = _dft_consts()[3][:n1]
    ctiles = c // HY_TC
    uspec = (pl.BlockSpec((n1 * FFT_R, HY_TC), lambda j, g: (0, j)) if u_natural
             else pl.BlockSpec((8, n1, HY_TC), lambda j, g: (g, 0, j)))
    return pl.pallas_call(
        functools.partial(_fft_ainv_kernel, n1=n1, u_natural=u_natural),
        grid=(ctiles, FFT_R // 8),
        in_specs=[pl.BlockSpec((None, FFT_R * 16, HY_TC), lambda j, g: (g, 0, j)),
                  uspec,
                  pl.BlockSpec((n1 * FFT_R, HY_TC), lambda j, g: (0, gate_col * ctiles + j)),
                  pl.BlockSpec((1, HY_TC), lambda j, g: (0, j)),
                  pl.BlockSpec((n1, 2 * FFT_R), lambda j, g: (0, 0))],
        out_specs=pl.BlockSpec((8, n1, HY_TC), lambda j, g: (g, 0, j)),
        out_shape=jax.ShapeDtypeStruct((FFT_R, n1, c), F32),
        compiler_params=_cparams(("arbitrary", "arbitrary")),
        name="fft_stage_a_inv",
    )(z, u, gate, skip.reshape(1, c), inv_a)


def hyena_long(u3, filt, skip):
    n1 = u3.shape[0] // FFT_R
    ct = HY_W // HY_TC
    hspec = fft_stage_b_spec(fft_stage_a(filt, FFT_R, True, 2 * HY_W))
    a = fft_stage_a(u3, n1, True, HY_W)
    y1 = fft_stage_a_inv(fft_stage_b_conv(a, hspec, 0), u3, u3, skip[0], 1, n1, True)
    a = fft_stage_a(y1, n1, False, HY_W)
    return fft_stage_a_inv(fft_stage_b_conv(a, hspec, ct), y1, u3, skip[1], 2, n1, False)


def _unpermute(y):
    return jnp.swapaxes(y, 0, 1).reshape(-1, y.shape[-1])


def _rope_tables(ctx_len, n_rows):
    nf = HEAD_DIM // 4
    inv = ROPE_BASE ** (-jnp.arange(nf, dtype=F32) / nf)
    rows = jnp.repeat(jnp.arange(n_rows, dtype=F32), GRID_W)
    cols = (jnp.arange(n_rows * GRID_W) % GRID_W).astype(F32)
    ang = jnp.concatenate([rows[:, None] * inv, cols[:, None] * inv], axis=-1)
    ang = jnp.concatenate([jnp.zeros((ctx_len, HEAD_DIM // 2), F32), ang], axis=0)
    cos_t, sin_t = jnp.cos(ang), jnp.sin(ang)
    return jnp.concatenate([cos_t, cos_t], axis=-1), jnp.concatenate([-sin_t, sin_t], axis=-1)


def _moe(h2, x, mod_gate, final_gain, layer, moe_p, row0, n_tok, ctx_len, final):
    router_w, router_b, w_gate, b_gate, w_up, b_up, w_down, b_down = moe_p
    router_w, router_b = router_w[layer], router_b[layer]
    top_idx_p, top_w_p = router(h2, router_w, router_b, row0, n_tok)
    top_idx = top_idx_p[:, :TOP_K]
    onehot = (top_idx[:, :, None] == jnp.arange(N_EXPERTS)[None, None, :]).astype(jnp.int32)
    per_tok = onehot.sum(axis=1)
    before = jnp.cumsum(per_tok, axis=0) - per_tok
    counts = per_tok.sum(axis=0)
    nblk = (counts + MOE_TM - 1) // MOE_TM
    blk_end = jnp.cumsum(nblk)
    pad_start = (blk_end - nblk) * MOE_TM
    dest = (pad_start[top_idx] + jnp.take_along_axis(before, top_idx, axis=1)).astype(jnp.int32)
    n_blocks = -(-(n_tok * TOP_K + N_EXPERTS * (MOE_TM - 1)) // MOE_TM)
    blk_e = jnp.minimum(jnp.searchsorted(blk_end, jnp.arange(n_blocks), side='right'), N_EXPERTS - 1)
    n_used = blk_end[-1:].astype(jnp.int32)
    blk_first = (blk_end - nblk)[blk_e]
    blk_valid = jnp.clip(counts[blk_e] - (jnp.arange(n_blocks) - blk_first) * MOE_TM, 0, MOE_TM).astype(jnp.int32)
    tok = jnp.broadcast_to(jnp.arange(n_tok, dtype=jnp.int32)[:, None], dest.shape)
    slot_tok = jnp.zeros((n_blocks * MOE_TM,), jnp.int32).at[dest.reshape(-1)].set(tok.reshape(-1))
    ys = expert_ffn(h2, row0, slot_tok, blk_e.astype(jnp.int32), n_used, blk_valid, layer, w_gate, b_gate, w_up, b_up,
                    w_down, b_down)
    return moe_combine(ys, dest.reshape(-1), top_w_p, x, mod_gate, final_gain, row0, n_tok, ctx_len, final)


def kernel(x, c, ctx, c_ctx, norm_mix_g, norm_ffn_g, final_norm_g, ada_w, ada_b, w_in, w_out, ret_decay_logit,
           hy_conv_w, hy_conv_b, hy_filt_w1, hy_filt_b1, hy_filt_freq, hy_filt_w2, hy_filt_b2, hy_filt_w3, hy_skip,
           hg_lb_logits, router_w, router_b, moe_w_gate, moe_b_gate, moe_w_up, moe_b_up, moe_w_down, moe_b_down):
    bsz, seq, d = x.shape
    assert bsz == 1
    ctx_len = ctx.shape[1]
    l = ctx_len + seq
    cos2, sin2 = _rope_tables(ctx_len, seq // GRID_W)
    cvec = jnp.zeros((8, d), F32).at[0].set(c_ctx).at[1].set(c[0])
    mod = adaln(cvec, ada_w, ada_b)
    xs = jnp.concatenate([ctx[0], x[0]], axis=0)
    p_lb = jax.nn.softmax(hg_lb_logits.astype(F32), axis=0)
    lb_cum = jnp.cumsum(p_lb, axis=0)
    for layer in range(DEPTH):
        last = layer == DEPTH - 1
        m = mod[layer].reshape(8, N_MOD, d)
        mc, ml = m[0], m[1]
        mod_in = jnp.stack([mc[0], mc[1], ml[0], ml[1], mc[0], mc[0], mc[0], mc[0]])
        z = in_proj(xs, norm_mix_g[layer], mod_in, w_in[layer].astype(BF16), ctx_len, tm=768, tn=768)
        log_gamma = jax.nn.log_sigmoid(ret_decay_logit[layer].astype(F32))
        o_ret = retention_dir(z, log_gamma, cos2, sin2, None, ctx_len, reverse=False)
        o_ret = retention_dir(z, log_gamma, cos2, sin2, o_ret, ctx_len, reverse=True)
        lb = lb_cum[layer] - p_lb[0]
        o_hg = hgrn2_dir(z, lb, None, ctx_len, reverse=False)
        o_hg = hgrn2_dir(z, lb, o_hg, ctx_len, reverse=True)
        filt_p = (hy_filt_w1[layer], hy_filt_b1[layer], hy_filt_freq[layer], hy_filt_w2[layer], hy_filt_b2[layer],
                  hy_filt_w3[layer])
        u_ctx, u_lat = hyena_pre(z, hy_conv_w[layer], hy_conv_b[layer], ctx_len)
        o_hy_l = _unpermute(hyena_long(u_lat, hyena_filters(seq, *filt_p), hy_skip[layer]))
        if last:
            o_hy_c = jnp.zeros((ctx_len, HY_W), F32)
        else:
            u_ctx = jnp.pad(u_ctx, ((0, seq - ctx_len), (0, 0)))
            o_hy_c = _unpermute(hyena_long(u_ctx, hyena_filters(ctx_len, *filt_p), hy_skip[layer]))[:ctx_len]
        o_hy = jnp.concatenate([o_hy_c, o_hy_l], axis=0)
        mod_out = jnp.stack([mc[2], ml[2], mc[3], mc[4], ml[3], ml[4], mc[0], mc[0]])
        xs, h2 = out_proj(o_ret, o_hy, o_hg, w_out[layer].astype(BF16), xs, norm_ffn_g[layer], mod_out, ctx_len,
                          tm=384)
        moe_p = (router_w, router_b, moe_w_gate, moe_b_gate, moe_w_up, moe_b_up, moe_w_down, moe_b_down)
        mod_gate = jnp.stack([mc[5], ml[5], mc[0], mc[0], mc[0], mc[0], mc[0], mc[0]])
        if last:
            return _moe(h2, xs, mod_gate, final_norm_g, layer, moe_p, ctx_len, seq, ctx_len, True)[None]
        xs = _moe(h2, xs, mod_gate, final_norm_g, layer, moe_p, 0, l, ctx_len, False)
```

```python
import functools
import math

import jax
import jax.numpy as jnp
import numpy as np
from jax import lax
from jax.experimental import pallas as pl
from jax.experimental.pallas import tpu as pltpu

F32 = jnp.float32
BF16 = jnp.bfloat16

D_MODEL = 2048
DEPTH = 2
GRID_W = 64
HEAD_DIM = 128
RET_W = 3 * D_MODEL // 8
RET_H = RET_W // HEAD_DIM
HY_W = D_MODEL // 4
HG_W = D_MODEL - RET_W - HY_W
HG_H = HG_W // HEAD_DIM
RET_COLS = 4 * RET_W
HY_COLS = 3 * HY_W
HG_COLS = 5 * HG_W
IN_W = RET_COLS + HY_COLS + HG_COLS
ROPE_BASE = 10000.0
EPS = 1e-6
N_MOD = 6
HY_BANDS = 16
HY_MAX_DECAY = math.log(1e-2) / 0.3
HY_MIN_DECAY = math.log(1e-2) / 1.5
N_EXPERTS = 32
TOP_K = 4
SWIGLU_LIMIT = 7.0
SWIGLU_ALPHA = 1.702

LANES = 128
VMEM_LIMIT = 56 * 1024 * 1024

RET_CHUNK = 256
RET_HEADS_PER_STEP = 3
HG_CHUNK = 128
HG_SUB = 16
HG_HEADS_PER_STEP = 3
MOE_TM = 576
GATHER_UNROLL = 8
MOE_TF = 512


def _cparams(sem):
    return pltpu.CompilerParams(dimension_semantics=sem, vmem_limit_bytes=VMEM_LIMIT)


def _dot(a, b):
    return jnp.dot(a, b, preferred_element_type=F32)


def _dot_nt(a, b):
    return lax.dot_general(a, b, (((1,), (1,)), ((), ())), preferred_element_type=F32)


def _adaln_kernel(c_ref, w_ref, b_ref, o_ref):
    c = c_ref[...]
    cond = c * jax.nn.sigmoid(c)
    o_ref[0] = _dot(cond.astype(BF16), w_ref[0].astype(BF16)) + b_ref[0]


def adaln(cvec, ada_w, ada_b, tn=1536):
    depth, d, n = ada_w.shape
    return pl.pallas_call(
        _adaln_kernel,
        grid=(depth, n // tn),
        in_specs=[pl.BlockSpec((8, d), lambda l, j: (0, 0)),
                  pl.BlockSpec((1, d, tn), lambda l, j: (l, 0, j)),
                  pl.BlockSpec((1, 1, tn), lambda l, j: (l, 0, j))],
        out_specs=pl.BlockSpec((1, 8, tn), lambda l, j: (l, 0, j)),
        out_shape=jax.ShapeDtypeStruct((depth, 8, n), F32),
        compiler_params=_cparams(("arbitrary", "arbitrary")),
        name="adaln",
    )(cvec, ada_w, ada_b.reshape(depth, 1, n))


def _norm_mod(x, gain, mod_ref, row0, ctx_len, shift_row):
    y = x * lax.rsqrt(jnp.mean(x * x, axis=-1, keepdims=True) + EPS) * gain
    row = row0 + lax.broadcasted_iota(jnp.int32, (x.shape[0], 1), 0)
    is_ctx = row < ctx_len
    shift = jnp.where(is_ctx, mod_ref[shift_row:shift_row + 1, :], mod_ref[shift_row + 2:shift_row + 3, :])
    scale = jnp.where(is_ctx, mod_ref[shift_row + 1:shift_row + 2, :], mod_ref[shift_row + 3:shift_row + 4, :])
    return y * (1.0 + scale) + shift


def _win_kernel(x_ref, g_ref, mod_ref, w_ref, o_ref, h_scr, *, tm, ctx_len):
    i = pl.program_id(0)

    @pl.when(pl.program_id(1) == 0)
    def _():
        h_scr[...] = _norm_mod(x_ref[...], g_ref[...], mod_ref, i * tm, ctx_len, 0).astype(BF16)

    o_ref[...] = _dot(h_scr[...], w_ref[...])


def in_proj(x, gain, mod, w_bf16, ctx_len, tm, tn):
    l, d = x.shape
    n = w_bf16.shape[1]
    return pl.pallas_call(
        functools.partial(_win_kernel, tm=tm, ctx_len=ctx_len),
        grid=(l // tm, n // tn),
        in_specs=[pl.BlockSpec((tm, d), lambda i, j: (i, 0)),
                  pl.BlockSpec((1, d), lambda i, j: (0, 0)),
                  pl.BlockSpec((8, d), lambda i, j: (0, 0)),
                  pl.BlockSpec((d, tn), lambda i, j: (0, j))],
        out_specs=pl.BlockSpec((tm, tn), lambda i, j: (i, j)),
        out_shape=jax.ShapeDtypeStruct((l, n), F32),
        scratch_shapes=[pltpu.VMEM((tm, d), BF16)],
        compiler_params=_cparams(("arbitrary", "arbitrary")),
        name="in_proj",
    )(x, gain.reshape(1, d), mod, w_bf16)


def _chunk_of_step(i, n_ctx, n_tot, reverse):
    if not reverse:
        return i
    return jnp.where(i < n_ctx, n_ctx - 1 - i, n_tot - 1 - (i - n_ctx))


def _head_norm_gate(o, gate):
    return gate * (o * lax.rsqrt(jnp.mean(o * o, axis=-1, keepdims=True) + EPS))


def _ret_kernel(lg_ref, q_ref, k_ref, v_ref, cos_ref, sin_ref, *rest, reverse, final, chunk, heads):
    if final:
        of_ref, g_ref, o_ref, st_scr = rest
    else:
        o_ref, st_scr = rest
    c = chunk

    @pl.when(pl.program_id(1) == 0)
    def _():
        st_scr[...] = jnp.zeros_like(st_scr)

    cos2 = cos_ref[...]
    sin2 = sin_ref[...]

    def rope(x):
        return x * cos2 + pltpu.roll(x, HEAD_DIM // 2, 1) * sin2

    t = lax.broadcasted_iota(jnp.int32, (c, c), 0)
    s = lax.broadcasted_iota(jnp.int32, (c, c), 1)
    dist = (s - t) if reverse else (t - s)
    a = lax.broadcasted_iota(jnp.int32, (c, 1), 0)
    steps_q = ((c - a) if reverse else (a + 1)).astype(F32)
    steps_k = (a if reverse else (c - 1 - a)).astype(F32)
    for hh in range(heads):
        cols = slice(hh * HEAD_DIM, (hh + 1) * HEAD_DIM)
        lg = lg_ref[1 if reverse else 0, pl.program_id(0) * heads + hh]
        q = rope(q_ref[:, cols]).astype(BF16)
        k = rope(k_ref[:, cols]) * (HEAD_DIM ** -0.5)
        v = v_ref[:, cols]
        dec = jnp.where(dist >= 0, jnp.exp(lg * jnp.maximum(dist, 0).astype(F32)), 0.0)
        p = (_dot_nt(q, k.astype(BF16)) * dec).astype(BF16)
        st = st_scr[hh]
        o = _dot(p, v.astype(BF16)) + jnp.exp(lg * steps_q) * _dot_nt(q, st.astype(BF16))
        kd = (k * jnp.exp(lg * steps_k)).astype(BF16)
        st_scr[hh] = jnp.exp(lg * c) * st + _dot(v.T.astype(BF16), kd)
        if final:
            g = g_ref[:, cols]
            o_ref[:, cols] = _head_norm_gate(o + of_ref[:, cols], g * jax.nn.sigmoid(g)).astype(o_ref.dtype)
        else:
            o_ref[:, cols] = o


def retention_dir(z, log_gamma, cos2, sin2, o_fwd, ctx_len, reverse):
    l = z.shape[0]
    c, heads = RET_CHUNK, RET_HEADS_PER_STEP
    w = heads * HEAD_DIM
    n_tot, n_ctx = l // c, ctx_len // c
    hb = RET_W // w

    def zspec(col):
        return pl.BlockSpec((c, w), lambda h, i: (_chunk_of_step(i, n_ctx, n_tot, reverse), col + h))

    tspec = pl.BlockSpec((c, HEAD_DIM), lambda h, i: (_chunk_of_step(i, n_ctx, n_tot, reverse), 0))
    ospec = pl.BlockSpec((c, w), lambda h, i: (_chunk_of_step(i, n_ctx, n_tot, reverse), h))
    final = o_fwd is not None
    in_specs = [pl.BlockSpec(memory_space=pltpu.SMEM), zspec(0), zspec(hb), zspec(2 * hb), tspec, tspec]
    args = [log_gamma, z, z, z, cos2, sin2]
    if final:
        in_specs += [ospec, zspec(3 * hb)]
        args += [o_fwd, z]
    return pl.pallas_call(
        functools.partial(_ret_kernel, reverse=reverse, final=final, chunk=c, heads=heads),
        grid=(hb, n_tot),
        in_specs=in_specs,
        out_specs=ospec,
        out_shape=jax.ShapeDtypeStruct((l, RET_W), BF16 if final else F32),
        scratch_shapes=[pltpu.VMEM((heads, HEAD_DIM, HEAD_DIM), F32)],
        compiler_params=_cparams(("arbitrary", "arbitrary")),
        name="retention_bwd" if reverse else "retention_fwd",
    )(*args)


def _hg_kernel(q_ref, fz_ref, v_ref, lb_ref, tri_ref, sel_ref, *rest, reverse, final, chunk, sub, heads):
    if final:
        of_ref, g_ref, o_ref, st_scr, cum_scr, k_scr, p_scr = rest
    else:
        o_ref, st_scr, cum_scr, k_scr, p_scr = rest
    c = chunk

    @pl.when(pl.program_id(1) == 0)
    def _():
        st_scr[...] = jnp.zeros_like(st_scr)

    row = lax.broadcasted_iota(jnp.int32, (c, 1), 0)
    t = lax.broadcasted_iota(jnp.int32, (c, c), 0)
    s = lax.broadcasted_iota(jnp.int32, (c, c), 1)

    for hh in range(heads):
        cols = slice(hh * HEAD_DIM, (hh + 1) * HEAD_DIM)
        cum_h, k_h, p_h = cum_scr.at[hh], k_scr.at[hh], p_scr.at[hh]
        qr = q_ref[:, cols]
        q = qr * jax.nn.sigmoid(qr)
        lb = lb_ref[0][:, cols]
        sg = jax.nn.sigmoid(fz_ref[:, cols])
        logf = jnp.log(lb + (1.0 - lb) * sg)
        kk = (1.0 - lb) * (1.0 - sg)
        v32 = v_ref[:, cols]
        v = v32.astype(BF16)
        cum = jnp.dot(tri_ref[...], logf, precision=lax.Precision.HIGHEST, preferred_element_type=F32)
        cum_h[...] = cum
        k_h[...] = kk

        a_mat = jnp.zeros((c, c), F32)
        m = c // 2
        while m >= sub:
            parts = []
            for p in range(c // (2 * m)):
                r = p * 2 * m + (m if reverse else m - 1)
                parts.append(jnp.broadcast_to(cum_h[pl.ds(r, 1), :], (2 * m, HEAD_DIM)))
            dq = cum - (jnp.concatenate(parts, axis=0) if len(parts) > 1 else parts[0])
            second_half = ((row >> (m.bit_length() - 1)) & 1) == 1
            is_q = jnp.logical_not(second_half) if reverse else second_half
            eq = jnp.where(is_q, jnp.exp(jnp.minimum(dq, 0.0)), 0.0)
            ek = jnp.where(is_q, 0.0, jnp.exp(jnp.minimum(-dq, 0.0)))
            a_l = _dot_nt((q * eq).astype(BF16), (kk * ek).astype(BF16))
            a_mat = a_mat + jnp.where((t >> m.bit_length()) == (s >> m.bit_length()), a_l, 0.0)
            m //= 2

        for si in range(c):
            r0 = (si // sub) * sub
            j = si % sub
            rows = r0 + lax.broadcasted_iota(jnp.int32, (sub, 1), 0)
            keep = (rows <= si) if reverse else (rows >= si)
            e = jnp.exp(jnp.minimum(cum[r0:r0 + sub, :] - cum_h[pl.ds(si, 1), :], 0.0))
            pv = jnp.where(keep, q[r0:r0 + sub, :] * k_h[pl.ds(si, 1), :] * e, 0.0)
            p_h[r0:r0 + sub, j * HEAD_DIM:(j + 1) * HEAD_DIM] = pv.astype(BF16)
        a_diag = _dot(p_h[...], sel_ref[...])
        pieces = []
        for b in range(c // sub):
            blk = a_diag[b * sub:(b + 1) * sub, :]
            pieces.append(blk if b == 0 else pltpu.roll(blk, b * sub, 1))
        a_mat = a_mat + jnp.concatenate(pieces, axis=0)

        st = st_scr[hh]
        o = _dot(a_mat.astype(BF16), v) + _dot_nt((q * jnp.exp(cum)).astype(BF16), st.astype(BF16))
        last = cum_h[pl.ds(0 if reverse else c - 1, 1), :]
        kd = (kk * jnp.exp(last - cum)).astype(BF16)
        st_scr[hh] = st * jnp.exp(last) + _dot(v32.T.astype(BF16), kd)
        if final:
            o_ref[:, cols] = _head_norm_gate(o + of_ref[:, cols], jax.nn.sigmoid(g_ref[:, cols])).astype(o_ref.dtype)
        else:
            o_ref[:, cols] = o


def hgrn2_dir(z, lb, o_fwd, ctx_len, reverse):
    l = z.shape[0]
    c, sub, heads = HG_CHUNK, HG_SUB, HG_HEADS_PER_STEP
    w = heads * HEAD_DIM
    n_tot, n_ctx = l // c, ctx_len // c
    col0 = (RET_COLS + HY_COLS) // w
    hb = HG_W // w
    final = o_fwd is not None
    d = 1 if reverse else 0

    def zspec(col):
        return pl.BlockSpec((c, w), lambda h, i: (_chunk_of_step(i, n_ctx, n_tot, reverse), col0 + col + h))

    ospec = pl.BlockSpec((c, w), lambda h, i: (_chunk_of_step(i, n_ctx, n_tot, reverse), h))
    r = np.arange(c)
    tri = (r[None, :] >= r[:, None]) if reverse else (r[None, :] <= r[:, None])
    sel = (np.arange(sub * HEAD_DIM)[:, None] // HEAD_DIM) == np.arange(HEAD_DIM)[None, :]
    in_specs = [zspec(0), zspec((1 + d) * hb), zspec(3 * hb),
                pl.BlockSpec((1, 1, w), lambda h, i: (d * hb + h, 0, 0)),
                pl.BlockSpec((c, c), lambda h, i: (0, 0)),
                pl.BlockSpec((sub * HEAD_DIM, HEAD_DIM), lambda h, i: (0, 0))]
    args = [z, z, z, lb.reshape(2 * hb, 1, w), jnp.asarray(tri, F32), jnp.asarray(sel, BF16)]
    if final:
        in_specs += [ospec, zspec(4 * hb)]
        args += [o_fwd, z]
    return pl.pallas_call(
        functools.partial(_hg_kernel, reverse=reverse, final=final, chunk=c, sub=sub, heads=heads),
        grid=(hb, n_tot),
        in_specs=in_specs,
        out_specs=ospec,
        out_shape=jax.ShapeDtypeStruct((l, HG_W), BF16 if final else F32),
        scratch_shapes=[pltpu.VMEM((heads, HEAD_DIM, HEAD_DIM), F32), pltpu.VMEM((heads, c, HEAD_DIM), F32),
                        pltpu.VMEM((heads, c, HEAD_DIM), F32), pltpu.VMEM((heads, c, sub * HEAD_DIM), BF16)],
        compiler_params=_cparams(("arbitrary", "arbitrary")),
        name="hgrn2_bwd" if reverse else "hgrn2_fwd",
    )(*args)


def _wout_kernel(oret_ref, ohy_ref, ohg_ref, w_ref, x_ref, g_ref, mod_ref, xo_ref, h_ref, *, tm, ctx_len):
    i = pl.program_id(0)
    y = (_dot(oret_ref[...], w_ref[0:RET_W, :]) + _dot(ohy_ref[...].astype(BF16), w_ref[RET_W:RET_W + HY_W, :])
         + _dot(ohg_ref[...], w_ref[RET_W + HY_W:, :]))
    row = i * tm + lax.broadcasted_iota(jnp.int32, (tm, 1), 0)
    gate = jnp.where(row < ctx_len, mod_ref[0:1, :], mod_ref[1:2, :])
    x = x_ref[...] + gate * y
    xo_ref[...] = x
    h_ref[...] = _norm_mod(x, g_ref[...], mod_ref, i * tm, ctx_len, 2)


def out_proj(o_ret, o_hy, o_hg, w_bf16, x, gain, mod, ctx_len, tm):
    l, d = x.shape
    return pl.pallas_call(
        functools.partial(_wout_kernel, tm=tm, ctx_len=ctx_len),
        grid=(l // tm,),
        in_specs=[pl.BlockSpec((tm, RET_W), lambda i: (i, 0)),
                  pl.BlockSpec((tm, HY_W), lambda i: (i, 0)),
                  pl.BlockSpec((tm, HG_W), lambda i: (i, 0)),
                  pl.BlockSpec((d, d), lambda i: (0, 0)),
                  pl.BlockSpec((tm, d), lambda i: (i, 0)),
                  pl.BlockSpec((1, d), lambda i: (0, 0)),
                  pl.BlockSpec((8, d), lambda i: (0, 0))],
        out_specs=[pl.BlockSpec((tm, d), lambda i: (i, 0)), pl.BlockSpec((tm, d), lambda i: (i, 0))],
        out_shape=[jax.ShapeDtypeStruct((l, d), F32), jax.ShapeDtypeStruct((l, d), F32)],
        compiler_params=_cparams(("arbitrary",)),
        name="out_proj",
    )(o_ret, o_hy, o_hg, w_bf16, x, gain.reshape(1, d), mod)


def _router_kernel(h_ref, w_ref, b_ref, idx_ref, wt_ref):
    logits = _dot(h_ref[...].astype(BF16), w_ref[...]) + b_ref[...]
    lane = lax.broadcasted_iota(jnp.int32, logits.shape, 1).astype(F32)
    vals, idxs = [], []
    x = logits
    for _ in range(TOP_K):
        mx = jnp.max(x, axis=-1, keepdims=True)
        ix = jnp.min(jnp.where(x == mx, lane, float(LANES)), axis=-1, keepdims=True)
        vals.append(mx)
        idxs.append(ix)
        x = jnp.where(lane == ix, -jnp.inf, x)
    es = [jnp.exp(vv - vals[0]) for vv in vals]
    tot = es[0] + es[1] + es[2] + es[3]
    idx_out = jnp.zeros(logits.shape, F32)
    wt_out = jnp.zeros(logits.shape, F32)
    for r in range(TOP_K):
        idx_out = jnp.where(lane == float(r), idxs[r], idx_out)
        wt_out = jnp.where(lane == float(r), es[r] / tot, wt_out)
    idx_ref[...] = idx_out.astype(jnp.int32)
    wt_ref[...] = wt_out


def router(h2, router_w, router_b, row0, n_tok, tm=256):
    d = h2.shape[1]
    w = jnp.zeros((d, LANES), BF16).at[:, :N_EXPERTS].set(router_w.astype(BF16))
    b = jnp.full((1, LANES), -jnp.inf, F32).at[0, :N_EXPERTS].set(router_b)
    return pl.pallas_call(
        _router_kernel,
        grid=(n_tok // tm,),
        in_specs=[pl.BlockSpec((tm, d), lambda i: (i + row0 // tm, 0)),
                  pl.BlockSpec((d, LANES), lambda i: (0, 0)),
                  pl.BlockSpec((1, LANES), lambda i: (0, 0))],
        out_specs=[pl.BlockSpec((tm, LANES), lambda i: (i, 0)), pl.BlockSpec((tm, LANES), lambda i: (i, 0))],
        out_shape=[jax.ShapeDtypeStruct((n_tok, LANES), jnp.int32), jax.ShapeDtypeStruct((n_tok, LANES), F32)],
        compiler_params=_cparams(("arbitrary",)),
        name="router",
    )(h2, w, b)


def _row_copy(src_hbm, src_row, dst, dst_row, sem):
    return pltpu.make_async_copy(src_hbm.at[pl.ds(src_row, 1)], dst.at[pl.ds(dst_row, 1)], sem)


def _expert_kernel(blk_e_ref, nused_ref, valid_ref, tok_ref, h_hbm, wg_ref, bg_ref, wu_ref, bu_ref, wd_ref, bd_ref,
                   o_ref, xf_scr, xb_scr, sem, *, row0):
    b = pl.program_id(0)
    f = pl.program_id(1)
    nused = nused_ref[0]

    def gather(blk, wait):
        def body(g, carry):
            for u in range(GATHER_UNROLL):
                r = g * GATHER_UNROLL + u
                cp = _row_copy(h_hbm, row0 + tok_ref[blk * MOE_TM + r], xf_scr, r, sem)
                if wait:
                    cp.wait()
                else:
                    cp.start()
            return carry

        n_groups = lax.shift_right_logical(valid_ref[blk] + (GATHER_UNROLL - 1), GATHER_UNROLL.bit_length() - 1)
        lax.fori_loop(0, n_groups, body, 0)

    @pl.when(b < nused)
    def _():
        @pl.when(f == 0)
        def _():
            @pl.when(b == 0)
            def _():
                xf_scr[...] = jnp.zeros_like(xf_scr)
                gather(0, False)

            gather(b, True)
            xb_scr[...] = xf_scr[...].astype(BF16)

        @pl.when((f == 1) & (b + 1 < nused))
        def _():
            gather(b + 1, False)

        x = xb_scr[...]
        gate = jnp.minimum(_dot(x, wg_ref[0].astype(BF16)) + bg_ref[0], SWIGLU_LIMIT)
        up = jnp.clip(_dot(x, wu_ref[0].astype(BF16)) + bu_ref[0], -SWIGLU_LIMIT, SWIGLU_LIMIT)
        act = gate * jax.nn.sigmoid(SWIGLU_ALPHA * gate) * (up + 1.0)
        part = _dot(act.astype(BF16), wd_ref[0].astype(BF16))

        @pl.when(f == 0)
        def _():
            o_ref[...] = part + bd_ref[0]

        @pl.when(f != 0)
        def _():
            o_ref[...] += part

    @pl.when((b >= nused) & (f == 0))
    def _():
        o_ref[...] = jnp.zeros_like(o_ref)


def expert_ffn(h2, row0, slot_tok, blk_e, n_used, blk_valid, layer, w_gate, b_gate, w_up, b_up, w_down, b_down):
    d = h2.shape[1]
    n_rows = slot_tok.shape[0]
    depth, e, _, dff = w_gate.shape
    nb, nf = n_rows // MOE_TM, dff // MOE_TF
    assert nf >= 2 and MOE_TM % GATHER_UNROLL == 0

    def blk(b, nu):
        return jnp.minimum(b, nu[0] - 1)

    def fidx(b, f, nu):
        return jnp.where(b < nu[0], f, nf - 1)

    grid_spec = pltpu.PrefetchScalarGridSpec(
        num_scalar_prefetch=4,
        grid=(nb, nf),
        in_specs=[pl.BlockSpec(memory_space=pl.ANY),
                  pl.BlockSpec((None, 1, d, MOE_TF), lambda b, f, be, nu, vl, tk: (layer, be[blk(b, nu)], 0, fidx(b, f, nu))),
                  pl.BlockSpec((None, 1, 1, MOE_TF), lambda b, f, be, nu, vl, tk: (layer, be[blk(b, nu)], 0, fidx(b, f, nu))),
                  pl.BlockSpec((None, 1, d, MOE_TF), lambda b, f, be, nu, vl, tk: (layer, be[blk(b, nu)], 0, fidx(b, f, nu))),
                  pl.BlockSpec((None, 1, 1, MOE_TF), lambda b, f, be, nu, vl, tk: (layer, be[blk(b, nu)], 0, fidx(b, f, nu))),
                  pl.BlockSpec((None, 1, MOE_TF, d), lambda b, f, be, nu, vl, tk: (layer, be[blk(b, nu)], fidx(b, f, nu), 0)),
                  pl.BlockSpec((None, 1, 1, d), lambda b, f, be, nu, vl, tk: (layer, be[blk(b, nu)], 0, 0))],
        out_specs=pl.BlockSpec((MOE_TM, d), lambda b, f, be, nu, vl, tk: (b, 0)),
        scratch_shapes=[pltpu.VMEM((MOE_TM, d), F32), pltpu.VMEM((MOE_TM, d), BF16), pltpu.SemaphoreType.DMA],
    )
    return pl.pallas_call(
        functools.partial(_expert_kernel, row0=row0),
        grid_spec=grid_spec,
        out_shape=jax.ShapeDtypeStruct((n_rows, d), F32),
        compiler_params=_cparams(("arbitrary", "arbitrary")),
        name="expert_ffn",
    )(blk_e, n_used, blk_valid, slot_tok, h2, w_gate, b_gate.reshape(depth, e, 1, dff), w_up,
      b_up.reshape(depth, e, 1, dff), w_down, b_down.reshape(depth, e, 1, d))


def _combine_kernel(dest_ref, y_hbm, wt_ref, x_ref, mod_ref, g_ref, o_ref, y_scr, sem, *, tm, row0, ctx_len, final):
    i = pl.program_id(0)

    def gather(tile, wait):
        def body(r, carry):
            for k in range(TOP_K):
                cp = _row_copy(y_hbm, dest_ref[(tile * tm + r) * TOP_K + k], y_scr.at[tile % 2, k], r, sem.at[tile % 2])
                if wait:
                    cp.wait()
                else:
                    cp.start()
            return carry

        lax.fori_loop(0, tm, body, 0, unroll=4)

    @pl.when(i == 0)
    def _():
        gather(0, False)

    @pl.when(i + 1 < pl.num_programs(0))
    def _():
        gather(i + 1, False)

    gather(i, True)
    wt = wt_ref[...]
    ys = y_scr.at[i % 2]
    acc = ys[0] * wt[:, 0:1]
    for k in range(1, TOP_K):
        acc = acc + ys[k] * wt[:, k:k + 1]
    row = row0 + i * tm + lax.broadcasted_iota(jnp.int32, (tm, 1), 0)
    x = x_ref[...] + jnp.where(row < ctx_len, mod_ref[0:1, :], mod_ref[1:2, :]) * acc
    if final:
        x = x * lax.rsqrt(jnp.mean(x * x, axis=-1, keepdims=True) + EPS) * g_ref[...]
    o_ref[...] = x


def moe_combine(ys, dest, top_w, x, mod, final_gain, row0, n_tok, ctx_len, final, tm=128):
    d = x.shape[1]
    grid_spec = pltpu.PrefetchScalarGridSpec(
        num_scalar_prefetch=1,
        grid=(n_tok // tm,),
        in_specs=[pl.BlockSpec(memory_space=pl.ANY),
                  pl.BlockSpec((tm, LANES), lambda i, ds: (i, 0)),
                  pl.BlockSpec((tm, d), lambda i, ds: (i + row0 // tm, 0)),
                  pl.BlockSpec((8, d), lambda i, ds: (0, 0)),
                  pl.BlockSpec((1, d), lambda i, ds: (0, 0))],
        out_specs=pl.BlockSpec((tm, d), lambda i, ds: (i, 0)),
        scratch_shapes=[pltpu.VMEM((2, TOP_K, tm, d), F32), pltpu.SemaphoreType.DMA((2,))],
    )
    return pl.pallas_call(
        functools.partial(_combine_kernel, tm=tm, row0=row0, ctx_len=ctx_len, final=final),
        grid_spec=grid_spec,
        out_shape=jax.ShapeDtypeStruct((n_tok, d), F32),
        compiler_params=_cparams(("arbitrary",)),
        name="moe_combine",
    )(dest, ys, top_w, x, mod, final_gain.reshape(1, d))


FFT_R = 128
FFT_N = FFT_R * FFT_R
HY_TC = 128


def _hypre_kernel(z_ref, zp_ref, zn_ref, w_ref, b_ref, oc_ref, ol_ref, *, tm, ctx_len, l_tot):
    i = pl.program_id(1)
    x = z_ref[...]
    loc = lax.broadcasted_iota(jnp.int32, (tm, 1), 0)
    row = i * tm + loc
    prev = jnp.where(loc == 0, zp_ref[7:8, :], pltpu.roll(x, 1, 0))
    prev = jnp.where((row == 0) | (row == ctx_len), 0.0, prev)
    nxt = jnp.where(loc == tm - 1, zn_ref[0:1, :], pltpu.roll(x, tm - 1, 0))
    nxt = jnp.where((row == ctx_len - 1) | (row == l_tot - 1), 0.0, nxt)
    y = w_ref[0:1, :] * prev + w_ref[1:2, :] * x + w_ref[2:3, :] * nxt + b_ref[...]

    @pl.when(i == 0)
    def _():
        oc_ref[...] = y

    @pl.when(i > 0)
    def _():
        ol_ref[...] = y


def hyena_pre(z, conv_w, conv_b, ctx_len):
    l = z.shape[0]
    tm, tc = ctx_len, 512
    cb0 = RET_COLS // tc
    nr8 = l // 8
    return pl.pallas_call(
        functools.partial(_hypre_kernel, tm=tm, ctx_len=ctx_len, l_tot=l),
        grid=(HY_COLS // tc, l // tm),
        in_specs=[pl.BlockSpec((tm, tc), lambda c, i: (i, cb0 + c)),
                  pl.BlockSpec((8, tc), lambda c, i: (jnp.maximum(i * (tm // 8) - 1, 0), cb0 + c)),
                  pl.BlockSpec((8, tc), lambda c, i: (jnp.minimum((i + 1) * (tm // 8), nr8 - 1), cb0 + c)),
                  pl.BlockSpec((3, tc), lambda c, i: (0, c)),
                  pl.BlockSpec((1, tc), lambda c, i: (0, c))],
        out_specs=[pl.BlockSpec((tm, tc), lambda c, i: (0, c)),
                   pl.BlockSpec((tm, tc), lambda c, i: (jnp.maximum(i - 1, 0), c))],
        out_shape=[jax.ShapeDtypeStruct((ctx_len, HY_COLS), F32), jax.ShapeDtypeStruct((l - ctx_len, HY_COLS), F32)],
        compiler_params=_cparams(("arbitrary", "arbitrary")),
        name="hyena_pre",
    )(z, z, z, conv_w, conv_b.reshape(1, HY_COLS))


def _hyfilt_kernel(ft_ref, aux_ref, w1_ref, b1_ref, fr_ref, w2_ref, b2_ref, w3_ref, dl_ref, o_ref):
    fr = fr_ref[...]
    h = jnp.sin(fr * (_dot(ft_ref[...].astype(BF16), w1_ref[...].astype(BF16)) + b1_ref[...]))
    h = jnp.sin(fr * (_dot(h.astype(BF16), w2_ref[...].astype(BF16)) + b2_ref[...]))
    h = _dot(h.astype(BF16), w3_ref[...].astype(BF16))
    aux = aux_ref[...]
    t, m_f, m_b = aux[:, 0:1], aux[:, 1:2], aux[:, 2:3]
    window = jnp.exp(-t * dl_ref[...])
    for o in range(2):
        base = o * 2 * HY_W
        o_ref[:, o * HY_W:(o + 1) * HY_W] = (m_f * h[:, base:base + HY_W] + m_b * h[:, base + HY_W:base + 2 * HY_W]) * window


def hyena_filters(seq_len, circle, w1, b1, freq, w2, b2, w3):
    L = seq_len
    if circle:
        r = np.arange(FFT_N)
        is_f, is_b = r < L, r > FFT_N - L
        pos = np.where(is_f, r, np.where(is_b, FFT_N - r, 0))
    else:
        lag = np.arange(2 * L) - L
        is_f, is_b = lag >= 0, (lag < 0) & (lag > -L)
        pos = np.where(is_b, -lag, np.where(is_f, lag, 0))
    n_rows = pos.shape[0]
    t_lin = np.linspace(0.0, 1.0, L, dtype=np.float32)[:, None]
    w_ang = np.float32(2.0 * math.pi) * np.arange(L, dtype=np.float32)[:, None] / np.float32(L)
    f = np.linspace(1e-4, HY_BANDS - 1, HY_BANDS, dtype=np.float32)[None, :]
    feats = np.concatenate([t_lin, np.cos(f * w_ang), -np.sin(f * w_ang)], axis=-1)[pos]
    ne, nw = feats.shape[1], w1.shape[1]
    feats = np.pad(feats, ((0, 0), (0, LANES - ne)))
    aux = np.zeros((n_rows, LANES), np.float32)
    aux[:, 0], aux[:, 1], aux[:, 2] = t_lin[pos, 0], is_f, is_b
    w1p = jnp.pad(w1, ((0, LANES - ne), (0, LANES - nw)))
    w2p = jnp.pad(w2, ((0, LANES - nw), (0, LANES - nw)))
    w3p = jnp.pad(w3, ((0, LANES - nw), (0, 0)))
    pad1 = lambda a: jnp.pad(a, (0, LANES - nw)).reshape(1, LANES)
    deltas = np.abs(np.linspace(HY_MIN_DECAY, HY_MAX_DECAY, HY_W, dtype=np.float32)).reshape(1, HY_W)
    tm = min(1024, n_rows)
    full = lambda shape: pl.BlockSpec(shape, lambda i: (0,) * len(shape))
    return pl.pallas_call(
        _hyfilt_kernel,
        grid=(n_rows // tm,),
        in_specs=[pl.BlockSpec((tm, LANES), lambda i: (i, 0)), pl.BlockSpec((tm, LANES), lambda i: (i, 0)),
                  full((LANES, LANES)), full((1, LANES)), full((1, LANES)), full((LANES, LANES)), full((1, LANES)),
                  full((LANES, 4 * HY_W)), full((1, HY_W))],
        out_specs=pl.BlockSpec((tm, 2 * HY_W), lambda i: (i, 0)),
        out_shape=jax.ShapeDtypeStruct((n_rows, 2 * HY_W), F32),
        compiler_params=_cparams(("arbitrary",)),
        name="hyena_filters",
    )(jnp.asarray(feats), jnp.asarray(aux), w1p, pad1(b1), pad1(freq), w2p, pad1(b2), w3p, jnp.asarray(deltas))


def _dft_consts():
    k = np.arange(FFT_R)
    ang = 2.0 * np.pi * np.outer(k, k) / FFT_R
    c, s = np.cos(ang), np.sin(ang)
    fwd_a = np.concatenate([c, -s], axis=0)
    fwd_b = np.block([[c, s], [-s, c]])
    inv_b = np.block([[c, -s], [s, c]])
    inv_a = np.concatenate([c, -s], axis=1) / FFT_N
    tw = 2.0 * np.pi * np.outer(k, k) / FFT_N
    as32 = lambda a: jnp.asarray(a, F32)
    return (as32(fwd_a), as32(fwd_b), as32(inv_b), as32(inv_a),
            as32(np.cos(tw))[:, :, None], as32(-np.sin(tw))[:, :, None])


def _store_grouped(o_ref, idx, res):
    for part in range(2):
        for g in range(FFT_R // 8):
            o_ref[g, idx, part] = res[part * FFT_R + g * 8:part * FFT_R + (g + 1) * 8, :]


def _fft_a_kernel(x_ref, f_ref, o_ref, *, n1, natural):
    f = f_ref[...].astype(BF16)
    for n2 in range(FFT_R):
        xs = x_ref[pl.ds(n2, n1, stride=FFT_R), :] if natural else x_ref[n2]
        _store_grouped(o_ref, n2, _dot(f, xs.astype(BF16)))


def fft_stage_a(x, n1, natural, c, col0=0):
    fwd_a = _dft_consts()[0][:, :n1]
    xspec = (pl.BlockSpec((n1 * FFT_R, HY_TC), lambda j: (0, col0 + j)) if natural
             else pl.BlockSpec((FFT_R, n1, HY_TC), lambda j: (0, 0, col0 + j)))
    out = pl.pallas_call(
        functools.partial(_fft_a_kernel, n1=n1, natural=natural),
        grid=(c // HY_TC,),
        in_specs=[xspec, pl.BlockSpec((2 * FFT_R, n1), lambda j: (0, 0))],
        out_specs=pl.BlockSpec((FFT_R // 8, FFT_R, 2, 8, HY_TC), lambda j: (0, 0, 0, 0, j)),
        out_shape=jax.ShapeDtypeStruct((FFT_R // 8, FFT_R, 2, 8, c), F32),
        compiler_params=_cparams(("arbitrary",)),
        name="fft_stage_a",
    )(x, fwd_a)
    return out.reshape(FFT_R // 8, FFT_R * 16, c)


def _load_pair(ref, j):
    return jnp.concatenate([ref[pl.ds(j, FFT_R, stride=16), :], ref[pl.ds(8 + j, FFT_R, stride=16), :]], axis=0)


def _twiddle(v, tr, ti):
    vr, vi = v[:FFT_R], v[FFT_R:]
    return jnp.concatenate([vr * tr - vi * ti, vr * ti + vi * tr], axis=0)


def _fft_b_spec_kernel(a_ref, twr_ref, twi_ref, fb_ref, o_ref):
    fb = fb_ref[...].astype(BF16)
    for kk in range(8):
        p = _twiddle(_load_pair(a_ref, kk), twr_ref[kk], twi_ref[kk])
        o_ref[kk] = _dot(fb, p.astype(BF16))


def _fft_b_conv_kernel(a_ref, h_ref, twr_ref, twi_ref, fb_ref, fib_ref, o_ref):
    fb = fb_ref[...].astype(BF16)
    fib = fib_ref[...].astype(BF16)
    for kk in range(8):
        tr, ti = twr_ref[kk], twi_ref[kk]
        x = _dot(fb, _twiddle(_load_pair(a_ref, kk), tr, ti).astype(BF16))
        h = h_ref[kk]
        y = _twiddle(x, h[:FFT_R], h[FFT_R:])
        z = _twiddle(_dot(fib, y.astype(BF16)), tr, -ti)
        for part in range(2):
            for g in range(FFT_R // 8):
                o_ref[g, kk, part] = z[part * FFT_R + g * 8:part * FFT_R + (g + 1) * 8, :]


def _bspecs(c):
    del c
    return [pl.BlockSpec((None, FFT_R * 16, HY_TC), lambda j, g: (g, 0, j))]


def fft_stage_b_spec(a):
    c = a.shape[-1]
    _, fwd_b, _, _, twr, twi = _dft_consts()
    tws = pl.BlockSpec((8, FFT_R, 1), lambda j, g: (g, 0, 0))
    return pl.pallas_call(
        _fft_b_spec_kernel,
        grid=(c // HY_TC, FFT_R // 8),
        in_specs=_bspecs(c) + [tws, tws, pl.BlockSpec((2 * FFT_R, 2 * FFT_R), lambda j, g: (0, 0))],
        out_specs=pl.BlockSpec((8, 2 * FFT_R, HY_TC), lambda j, g: (g, 0, j)),
        out_shape=jax.ShapeDtypeStruct((FFT_R, 2 * FFT_R, c), F32),
        compiler_params=_cparams(("arbitrary", "arbitrary")),
        name="fft_stage_b_spec",
    )(a, twr, twi, fwd_b)


def fft_stage_b_conv(a, hspec, h_col0):
    c = a.shape[-1]
    _, fwd_b, inv_b, _, twr, twi = _dft_consts()
    tws = pl.BlockSpec((8, FFT_R, 1), lambda j, g: (g, 0, 0))
    sq = pl.BlockSpec((2 * FFT_R, 2 * FFT_R), lambda j, g: (0, 0))
    out = pl.pallas_call(
        _fft_b_conv_kernel,
        grid=(c // HY_TC, FFT_R // 8),
        in_specs=_bspecs(c) + [pl.BlockSpec((8, 2 * FFT_R, HY_TC), lambda j, g: (g, 0, h_col0 + j)), tws, tws, sq, sq],
        out_specs=pl.BlockSpec((FFT_R // 8, 8, 2, 8, HY_TC), lambda j, g: (0, g, 0, 0, j)),
        out_shape=jax.ShapeDtypeStruct((FFT_R // 8, FFT_R, 2, 8, c), F32),
        compiler_params=_cparams(("arbitrary", "arbitrary")),
        name="fft_stage_b_conv",
    )(a, hspec, twr, twi, fwd_b, inv_b)
    return out.reshape(FFT_R // 8, FFT_R * 16, c)


def _fft_ainv_kernel(z_ref, u_ref, gate_ref, skip_ref, fi_ref, o_ref, *, n1, u_natural):
    g2 = pl.program_id(1)
    fi = fi_ref[...].astype(BF16)
    span = (n1 - 1) * FFT_R + 8
    base = pl.multiple_of(g2 * 8, 8)
    gate_rows = gate_ref.at[pl.ds(base, span)]
    u_rows = u_ref.at[pl.ds(base, span)] if u_natural else None
    for jj in range(8):
        y = _dot(fi, _load_pair(z_ref, jj).astype(BF16))
        u = u_rows[pl.ds(jj, n1, stride=FFT_R), :] if u_natural else u_ref[jj]
        o_ref[jj] = gate_rows[pl.ds(jj, n1, stride=FFT_R), :] * (y + skip_ref[...] * u)


def fft_stage_a_inv(z, u, gate, skip, gate_col, n1, u_natural):
    c = z.shape[-1]
    inv_a = _dft_consts()[3][:n1]
    ctiles = c // HY_TC
    uspec = (pl.BlockSpec((n1 * FFT_R, HY_TC), lambda j, g: (0, j)) if u_natural
             else pl.BlockSpec((8, n1, HY_TC), lambda j, g: (g, 0, j)))
    return pl.pallas_call(
        functools.partial(_fft_ainv_kernel, n1=n1, u_natural=u_natural),
        grid=(ctiles, FFT_R // 8),
        in_specs=[pl.BlockSpec((None, FFT_R * 16, HY_TC), lambda j, g: (g, 0, j)),
                  uspec,
                  pl.BlockSpec((n1 * FFT_R, HY_TC), lambda j, g: (0, gate_col * ctiles + j)),
                  pl.BlockSpec((1, HY_TC), lambda j, g: (0, j)),
                  pl.BlockSpec((n1, 2 * FFT_R), lambda j, g: (0, 0))],
        out_specs=pl.BlockSpec((8, n1, HY_TC), lambda j, g: (g, 0, j)),
        out_shape=jax.ShapeDtypeStruct((FFT_R, n1, c), F32),
        compiler_params=_cparams(("arbitrary", "arbitrary")),
        name="fft_stage_a_inv",
    )(z, u, gate, skip.reshape(1, c), inv_a)


def hyena_long(u3, filt, skip):
    n1 = u3.shape[0] // FFT_R
    ct = HY_W // HY_TC
    hspec = fft_stage_b_spec(fft_stage_a(filt, FFT_R, True, 2 * HY_W))
    a = fft_stage_a(u3, n1, True, HY_W)
    y1 = fft_stage_a_inv(fft_stage_b_conv(a, hspec, 0), u3, u3, skip[0], 1, n1, True)
    a = fft_stage_a(y1, n1, False, HY_W)
    return fft_stage_a_inv(fft_stage_b_conv(a, hspec, ct), y1, u3, skip[1], 2, n1, False)


def _hyctx_kernel(u_ref, h_ref, skip_ref, o_ref, vs_scr, *, seq_len):
    L = seq_len
    zeros = jnp.zeros((L, HY_TC), F32)

    def conv(u, order):
        padded = jnp.concatenate([zeros, u, zeros], axis=0)
        for b in range(8):
            vs_scr[b] = padded if b == 0 else pltpu.roll(padded, b, 0)

        def body(a, acc):
            taps = h_ref[pl.ds(pl.multiple_of(8 * a, 8), 8), order * HY_TC:(order + 1) * HY_TC]
            start = pl.multiple_of(2 * L - 8 * a, 8)
            for b in range(8):
                acc = acc + taps[b:b + 1, :] * vs_scr[b, pl.ds(start, L), :]
            return acc

        return lax.fori_loop(0, 2 * L // 8, body, zeros)

    v, x1, x2 = u_ref[:, 0:HY_TC], u_ref[:, HY_TC:2 * HY_TC], u_ref[:, 2 * HY_TC:3 * HY_TC]
    y = x1 * (conv(v, 0) + skip_ref[0:1, :] * v)
    o_ref[...] = x2 * (conv(y, 1) + skip_ref[1:2, :] * y)


def hyena_short(u3, lag_filt, skip):
    L = u3.shape[0]
    ct = HY_W // HY_TC
    u_t = u3.reshape(L, 3, ct, HY_TC).transpose(0, 2, 1, 3).reshape(L, ct * 3 * HY_TC)
    h_t = lag_filt.reshape(2 * L, 2, ct, HY_TC).transpose(0, 2, 1, 3).reshape(2 * L, ct * 2 * HY_TC)
    return pl.pallas_call(
        functools.partial(_hyctx_kernel, seq_len=L),
        grid=(ct,),
        in_specs=[pl.BlockSpec((L, 3 * HY_TC), lambda j: (0, j)),
                  pl.BlockSpec((2 * L, 2 * HY_TC), lambda j: (0, j)),
                  pl.BlockSpec((2, HY_TC), lambda j: (0, j))],
        out_specs=pl.BlockSpec((L, HY_TC), lambda j: (0, j)),
        out_shape=jax.ShapeDtypeStruct((L, HY_W), F32),
        scratch_shapes=[pltpu.VMEM((8, 3 * L, HY_TC), F32)],
        compiler_params=_cparams(("arbitrary",)),
        name="hyena_short",
    )(u_t, h_t, skip)


def _unpermute(y):
    return jnp.swapaxes(y, 0, 1).reshape(-1, y.shape[-1])


def _rope_tables(ctx_len, n_rows):
    nf = HEAD_DIM // 4
    inv = ROPE_BASE ** (-jnp.arange(nf, dtype=F32) / nf)
    rows = jnp.repeat(jnp.arange(n_rows, dtype=F32), GRID_W)
    cols = (jnp.arange(n_rows * GRID_W) % GRID_W).astype(F32)
    ang = jnp.concatenate([rows[:, None] * inv, cols[:, None] * inv], axis=-1)
    ang = jnp.concatenate([jnp.zeros((ctx_len, HEAD_DIM // 2), F32), ang], axis=0)
    cos_t, sin_t = jnp.cos(ang), jnp.sin(ang)
    return jnp.concatenate([cos_t, cos_t], axis=-1), jnp.concatenate([-sin_t, sin_t], axis=-1)


def _moe(h2, x, mod_gate, final_gain, layer, moe_p, row0, n_tok, ctx_len, final):
    router_w, router_b, w_gate, b_gate, w_up, b_up, w_down, b_down = moe_p
    router_w, router_b = router_w[layer], router_b[layer]
    top_idx_p, top_w_p = router(h2, router_w, router_b, row0, n_tok)
    top_idx = top_idx_p[:, :TOP_K]
    onehot = (top_idx[:, :, None] == jnp.arange(N_EXPERTS)[None, None, :]).astype(jnp.int32)
    per_tok = onehot.sum(axis=1)
    before = jnp.cumsum(per_tok, axis=0) - per_tok
    counts = per_tok.sum(axis=0)
    nblk = (counts + MOE_TM - 1) // MOE_TM
    blk_end = jnp.cumsum(nblk)
    pad_start = (blk_end - nblk) * MOE_TM
    dest = (pad_start[top_idx] + jnp.take_along_axis(before, top_idx, axis=1)).astype(jnp.int32)
    n_blocks = -(-(n_tok * TOP_K + N_EXPERTS * (MOE_TM - 1)) // MOE_TM)
    blk_e = jnp.minimum(jnp.searchsorted(blk_end, jnp.arange(n_blocks), side='right'), N_EXPERTS - 1)
    n_used = blk_end[-1:].astype(jnp.int32)
    blk_first = (blk_end - nblk)[blk_e]
    blk_valid = jnp.clip(counts[blk_e] - (jnp.arange(n_blocks) - blk_first) * MOE_TM, 0, MOE_TM).astype(jnp.int32)
    tok = jnp.broadcast_to(jnp.arange(n_tok, dtype=jnp.int32)[:, None], dest.shape)
    slot_tok = jnp.zeros((n_blocks * MOE_TM,), jnp.int32).at[dest.reshape(-1)].set(tok.reshape(-1))
    ys = expert_ffn(h2, row0, slot_tok, blk_e.astype(jnp.int32), n_used, blk_valid, layer, w_gate, b_gate, w_up, b_up,
                    w_down, b_down)
    return moe_combine(ys, dest.reshape(-1), top_w_p, x, mod_gate, final_gain, row0, n_tok, ctx_len, final)


def kernel(x, c, ctx, c_ctx, norm_mix_g, norm_ffn_g, final_norm_g, ada_w, ada_b, w_in, w_out, ret_decay_logit,
           hy_conv_w, hy_conv_b, hy_filt_w1, hy_filt_b1, hy_filt_freq, hy_filt_w2, hy_filt_b2, hy_filt_w3, hy_skip,
           hg_lb_logits, router_w, router_b, moe_w_gate, moe_b_gate, moe_w_up, moe_b_up, moe_w_down, moe_b_down):
    bsz, seq, d = x.shape
    assert bsz == 1
    ctx_len = ctx.shape[1]
    l = ctx_len + seq
    cos2, sin2 = _rope_tables(ctx_len, seq // GRID_W)
    cvec = jnp.zeros((8, d), F32).at[0].set(c_ctx).at[1].set(c[0])
    mod = adaln(cvec, ada_w, ada_b)
    xs = jnp.concatenate([ctx[0], x[0]], axis=0)
    p_lb = jax.nn.softmax(hg_lb_logits.astype(F32), axis=0)
    lb_cum = jnp.cumsum(p_lb, axis=0)
    for layer in range(DEPTH):
        last = layer == DEPTH - 1
        m = mod[layer].reshape(8, N_MOD, d)
        mc, ml = m[0], m[1]
        mod_in = jnp.stack([mc[0], mc[1], ml[0], ml[1], mc[0], mc[0], mc[0], mc[0]])
        z = in_proj(xs, norm_mix_g[layer], mod_in, w_in[layer].astype(BF16), ctx_len, tm=768, tn=768)
        log_gamma = jax.nn.log_sigmoid(ret_decay_logit[layer].astype(F32))
        o_ret = retention_dir(z, log_gamma, cos2, sin2, None, ctx_len, reverse=False)
        o_ret = retention_dir(z, log_gamma, cos2, sin2, o_ret, ctx_len, reverse=True)
        lb = lb_cum[layer] - p_lb[0]
        o_hg = hgrn2_dir(z, lb, None, ctx_len, reverse=False)
        o_hg = hgrn2_dir(z, lb, o_hg, ctx_len, reverse=True)
        filt_p = (hy_filt_w1[layer], hy_filt_b1[layer], hy_filt_freq[layer], hy_filt_w2[layer], hy_filt_b2[layer],
                  hy_filt_w3[layer])
        u_ctx, u_lat = hyena_pre(z, hy_conv_w[layer], hy_conv_b[layer], ctx_len)
        o_hy_l = _unpermute(hyena_long(u_lat, hyena_filters(seq, True, *filt_p), hy_skip[layer]))
        if last:
            o_hy_c = jnp.zeros((ctx_len, HY_W), F32)
        else:
            o_hy_c = hyena_short(u_ctx, hyena_filters(ctx_len, False, *filt_p), hy_skip[layer])
        o_hy = jnp.concatenate([o_hy_c, o_hy_l], axis=0)
        mod_out = jnp.stack([mc[2], ml[2], mc[3], mc[4], ml[3], ml[4], mc[0], mc[0]])
        xs, h2 = out_proj(o_ret, o_hy, o_hg, w_out[layer].astype(BF16), xs, norm_ffn_g[layer], mod_out, ctx_len,
                          tm=384)
        moe_p = (router_w, router_b, moe_w_gate, moe_b_gate, moe_w_up, moe_b_up, moe_w_down, moe_b_down)
        mod_gate = jnp.stack([mc[5], ml[5], mc[0], mc[0], mc[0], mc[0], mc[0], mc[0]])
        if last:
            return _moe(h2, xs, mod_gate, final_norm_g, layer, moe_p, ctx_len, seq, ctx_len, True)[None]
        xs = _moe(h2, xs, mod_gate, final_norm_g, layer, moe_p, 0, l, ctx_len, False)
```

```python
import functools
import math

import jax
import jax.numpy as jnp
import numpy as np
from jax import lax
from jax.experimental import pallas as pl
from jax.experimental.pallas import tpu as pltpu

F32 = jnp.float32
BF16 = jnp.bfloat16

D_MODEL = 2048
DEPTH = 2
GRID_W = 64
HEAD_DIM = 128
RET_W = 3 * D_MODEL // 8
RET_H = RET_W // HEAD_DIM
HY_W = D_MODEL // 4
HG_W = D_MODEL - RET_W - HY_W
HG_H = HG_W // HEAD_DIM
RET_COLS = 4 * RET_W
HY_COLS = 3 * HY_W
HG_COLS = 5 * HG_W
IN_W = RET_COLS + HY_COLS + HG_COLS
ROPE_BASE = 10000.0
EPS = 1e-6
N_MOD = 6
HY_BANDS = 16
HY_MAX_DECAY = math.log(1e-2) / 0.3
HY_MIN_DECAY = math.log(1e-2) / 1.5
N_EXPERTS = 32
TOP_K = 4
SWIGLU_LIMIT = 7.0
SWIGLU_ALPHA = 1.702

LANES = 128
VMEM_LIMIT = 56 * 1024 * 1024

RET_CHUNK = 256
RET_HEADS_PER_STEP = 3
HG_CHUNK = 128
HG_SUB = 16
HG_HEADS_PER_STEP = 3
MOE_TM = 576
GATHER_UNROLL = 8
MOE_TF = 512


def _cparams(sem):
    return pltpu.CompilerParams(dimension_semantics=sem, vmem_limit_bytes=VMEM_LIMIT)


def _dot(a, b):
    return jnp.dot(a, b, preferred_element_type=F32)


def _dot_nt(a, b):
    return lax.dot_general(a, b, (((1,), (1,)), ((), ())), preferred_element_type=F32)


def _adaln_kernel(c_ref, w_ref, b_ref, o_ref):
    c = c_ref[...]
    cond = c * jax.nn.sigmoid(c)
    o_ref[0] = _dot(cond.astype(BF16), w_ref[0].astype(BF16)) + b_ref[0]


def adaln(cvec, ada_w, ada_b, tn=1536):
    depth, d, n = ada_w.shape
    return pl.pallas_call(
        _adaln_kernel,
        grid=(depth, n // tn),
        in_specs=[pl.BlockSpec((8, d), lambda l, j: (0, 0)),
                  pl.BlockSpec((1, d, tn), lambda l, j: (l, 0, j)),
                  pl.BlockSpec((1, 1, tn), lambda l, j: (l, 0, j))],
        out_specs=pl.BlockSpec((1, 8, tn), lambda l, j: (l, 0, j)),
        out_shape=jax.ShapeDtypeStruct((depth, 8, n), F32),
        compiler_params=_cparams(("arbitrary", "arbitrary")),
        name="adaln",
    )(cvec, ada_w, ada_b.reshape(depth, 1, n))


def _norm_mod(x, gain, mod_ref, row0, ctx_len, shift_row):
    y = x * lax.rsqrt(jnp.mean(x * x, axis=-1, keepdims=True) + EPS) * gain
    row = row0 + lax.broadcasted_iota(jnp.int32, (x.shape[0], 1), 0)
    is_ctx = row < ctx_len
    shift = jnp.where(is_ctx, mod_ref[shift_row:shift_row + 1, :], mod_ref[shift_row + 2:shift_row + 3, :])
    scale = jnp.where(is_ctx, mod_ref[shift_row + 1:shift_row + 2, :], mod_ref[shift_row + 3:shift_row + 4, :])
    return y * (1.0 + scale) + shift


def _win_kernel(x_ref, g_ref, mod_ref, w_ref, o_ref, h_scr, *, tm, ctx_len):
    i = pl.program_id(0)

    @pl.when(pl.program_id(1) == 0)
    def _():
        h_scr[...] = _norm_mod(x_ref[...], g_ref[...], mod_ref, i * tm, ctx_len, 0).astype(BF16)

    o_ref[...] = _dot(h_scr[...], w_ref[...])


def in_proj(x, gain, mod, w_bf16, ctx_len, tm, tn):
    l, d = x.shape
    n = w_bf16.shape[1]
    return pl.pallas_call(
        functools.partial(_win_kernel, tm=tm, ctx_len=ctx_len),
        grid=(l // tm, n // tn),
        in_specs=[pl.BlockSpec((tm, d), lambda i, j: (i, 0)),
                  pl.BlockSpec((1, d), lambda i, j: (0, 0)),
                  pl.BlockSpec((8, d), lambda i, j: (0, 0)),
                  pl.BlockSpec((d, tn), lambda i, j: (0, j))],
        out_specs=pl.BlockSpec((tm, tn), lambda i, j: (i, j)),
        out_shape=jax.ShapeDtypeStruct((l, n), F32),
        scratch_shapes=[pltpu.VMEM((tm, d), BF16)],
        compiler_params=_cparams(("arbitrary", "arbitrary")),
        name="in_proj",
    )(x, gain.reshape(1, d), mod, w_bf16)


def _chunk_of_step(i, n_ctx, n_tot, reverse):
    if not reverse:
        return i
    return jnp.where(i < n_ctx, n_ctx - 1 - i, n_tot - 1 - (i - n_ctx))


def _head_norm_gate(o, gate):
    return gate * (o * lax.rsqrt(jnp.mean(o * o, axis=-1, keepdims=True) + EPS))


def _ret_kernel(lg_ref, q_ref, k_ref, v_ref, cos_ref, sin_ref, *rest, reverse, final, chunk, heads):
    if final:
        of_ref, g_ref, o_ref, st_scr = rest
    else:
        o_ref, st_scr = rest
    c = chunk

    @pl.when(pl.program_id(1) == 0)
    def _():
        st_scr[...] = jnp.zeros_like(st_scr)

    cos2 = cos_ref[...]
    sin2 = sin_ref[...]

    def rope(x):
        return x * cos2 + pltpu.roll(x, HEAD_DIM // 2, 1) * sin2

    t = lax.broadcasted_iota(jnp.int32, (c, c), 0)
    s = lax.broadcasted_iota(jnp.int32, (c, c), 1)
    dist = (s - t) if reverse else (t - s)
    a = lax.broadcasted_iota(jnp.int32, (c, 1), 0)
    steps_q = ((c - a) if reverse else (a + 1)).astype(F32)
    steps_k = (a if reverse else (c - 1 - a)).astype(F32)
    for hh in range(heads):
        cols = slice(hh * HEAD_DIM, (hh + 1) * HEAD_DIM)
        lg = lg_ref[1 if reverse else 0, pl.program_id(0) * heads + hh]
        q = rope(q_ref[:, cols]).astype(BF16)
        k = rope(k_ref[:, cols]) * (HEAD_DIM ** -0.5)
        v = v_ref[:, cols]
        dec = jnp.where(dist >= 0, jnp.exp(lg * jnp.maximum(dist, 0).astype(F32)), 0.0)
        p = (_dot_nt(q, k.astype(BF16)) * dec).astype(BF16)
        st = st_scr[hh]
        o = _dot(p, v.astype(BF16)) + jnp.exp(lg * steps_q) * _dot_nt(q, st.astype(BF16))
        kd = (k * jnp.exp(lg * steps_k)).astype(BF16)
        st_scr[hh] = jnp.exp(lg * c) * st + _dot(v.T.astype(BF16), kd)
        if final:
            g = g_ref[:, cols]
            o_ref[:, cols] = _head_norm_gate(o + of_ref[:, cols], g * jax.nn.sigmoid(g)).astype(o_ref.dtype)
        else:
            o_ref[:, cols] = o


def retention_dir(z, log_gamma, cos2, sin2, o_fwd, ctx_len, reverse):
    l = z.shape[0]
    c, heads = RET_CHUNK, RET_HEADS_PER_STEP
    w = heads * HEAD_DIM
    n_tot, n_ctx = l // c, ctx_len // c
    hb = RET_W // w

    def zspec(col):
        return pl.BlockSpec((c, w), lambda h, i: (_chunk_of_step(i, n_ctx, n_tot, reverse), col + h))

    tspec = pl.BlockSpec((c, HEAD_DIM), lambda h, i: (_chunk_of_step(i, n_ctx, n_tot, reverse), 0))
    ospec = pl.BlockSpec((c, w), lambda h, i: (_chunk_of_step(i, n_ctx, n_tot, reverse), h))
    final = o_fwd is not None
    in_specs = [pl.BlockSpec(memory_space=pltpu.SMEM), zspec(0), zspec(hb), zspec(2 * hb), tspec, tspec]
    args = [log_gamma, z, z, z, cos2, sin2]
    if final:
        in_specs += [ospec, zspec(3 * hb)]
        args += [o_fwd, z]
    return pl.pallas_call(
        functools.partial(_ret_kernel, reverse=reverse, final=final, chunk=c, heads=heads),
        grid=(hb, n_tot),
        in_specs=in_specs,
        out_specs=ospec,
        out_shape=jax.ShapeDtypeStruct((l, RET_W), BF16 if final else F32),
        scratch_shapes=[pltpu.VMEM((heads, HEAD_DIM, HEAD_DIM), F32)],
        compiler_params=_cparams(("arbitrary", "arbitrary")),
        name="retention_bwd" if reverse else "retention_fwd",
    )(*args)


def _hg_kernel(q_ref, fz_ref, v_ref, lb_ref, tri_ref, sel_ref, *rest, reverse, final, chunk, sub, heads):
    if final:
        of_ref, g_ref, o_ref, st_scr, cum_scr, k_scr, p_scr = rest
    else:
        o_ref, st_scr, cum_scr, k_scr, p_scr = rest
    c = chunk

    @pl.when(pl.program_id(1) == 0)
    def _():
        st_scr[...] = jnp.zeros_like(st_scr)

    row = lax.broadcasted_iota(jnp.int32, (c, 1), 0)
    t = lax.broadcasted_iota(jnp.int32, (c, c), 0)
    s = lax.broadcasted_iota(jnp.int32, (c, c), 1)

    for hh in range(heads):
        cols = slice(hh * HEAD_DIM, (hh + 1) * HEAD_DIM)
        cum_h, k_h, p_h = cum_scr.at[hh], k_scr.at[hh], p_scr.at[hh]
        qr = q_ref[:, cols]
        q = qr * jax.nn.sigmoid(qr)
        lb = lb_ref[0][:, cols]
        sg = jax.nn.sigmoid(fz_ref[:, cols])
        logf = jnp.log(lb + (1.0 - lb) * sg)
        kk = (1.0 - lb) * (1.0 - sg)
        v32 = v_ref[:, cols]
        v = v32.astype(BF16)
        tri = tri_ref[...]
        f_hi = logf.astype(BF16)
        rest = logf - f_hi.astype(F32)
        f_mid = rest.astype(BF16)
        f_lo = (rest - f_mid.astype(F32)).astype(BF16)
        cum = _dot(tri, f_hi) + _dot(tri, f_mid) + _dot(tri, f_lo)
        cum_h[...] = cum
        k_h[...] = kk

        a_mat = jnp.zeros((c, c), F32)
        m = c // 2
        while m >= sub:
            parts = []
            for p in range(c // (2 * m)):
                r = p * 2 * m + (m if reverse else m - 1)
                parts.append(jnp.broadcast_to(cum_h[pl.ds(r, 1), :], (2 * m, HEAD_DIM)))
            dq = cum - (jnp.concatenate(parts, axis=0) if len(parts) > 1 else parts[0])
            second_half = ((row >> (m.bit_length() - 1)) & 1) == 1
            is_q = jnp.logical_not(second_half) if reverse else second_half
            eq = jnp.where(is_q, jnp.exp(jnp.minimum(dq, 0.0)), 0.0)
            ek = jnp.where(is_q, 0.0, jnp.exp(jnp.minimum(-dq, 0.0)))
            a_l = _dot_nt((q * eq).astype(BF16), (kk * ek).astype(BF16))
            a_mat = a_mat + jnp.where((t >> m.bit_length()) == (s >> m.bit_length()), a_l, 0.0)
            m //= 2

        for si in range(c):
            r0 = (si // sub) * sub
            j = si % sub
            rows = r0 + lax.broadcasted_iota(jnp.int32, (sub, 1), 0)
            keep = (rows <= si) if reverse else (rows >= si)
            e = jnp.exp(cum[r0:r0 + sub, :] - cum_h[pl.ds(si, 1), :])
            p_h[r0:r0 + sub, j * HEAD_DIM:(j + 1) * HEAD_DIM] = jnp.where(
                keep, q[r0:r0 + sub, :] * k_h[pl.ds(si, 1), :] * e, 0.0).astype(BF16)
        a_diag = _dot(p_h[...], sel_ref[...])
        pieces = []
        for b in range(c // sub):
            blk = a_diag[b * sub:(b + 1) * sub, :]
            pieces.append(blk if b == 0 else pltpu.roll(blk, b * sub, 1))
        a_mat = a_mat + jnp.concatenate(pieces, axis=0)

        st = st_scr[hh]
        o = _dot(a_mat.astype(BF16), v) + _dot_nt((q * jnp.exp(cum)).astype(BF16), st.astype(BF16))
        last = cum_h[pl.ds(0 if reverse else c - 1, 1), :]
        kd = (kk * jnp.exp(last - cum)).astype(BF16)
        st_scr[hh] = st * jnp.exp(last) + _dot(v32.T.astype(BF16), kd)
        if final:
            o_ref[:, cols] = _head_norm_gate(o + of_ref[:, cols], jax.nn.sigmoid(g_ref[:, cols])).astype(o_ref.dtype)
        else:
            o_ref[:, cols] = o


def hgrn2_dir(z, lb, o_fwd, ctx_len, reverse):
    l = z.shape[0]
    c, sub, heads = HG_CHUNK, HG_SUB, HG_HEADS_PER_STEP
    w = heads * HEAD_DIM
    n_tot, n_ctx = l // c, ctx_len // c
    col0 = (RET_COLS + HY_COLS) // w
    hb = HG_W // w
    final = o_fwd is not None
    d = 1 if reverse else 0

    def zspec(col):
        return pl.BlockSpec((c, w), lambda h, i: (_chunk_of_step(i, n_ctx, n_tot, reverse), col0 + col + h))

    ospec = pl.BlockSpec((c, w), lambda h, i: (_chunk_of_step(i, n_ctx, n_tot, reverse), h))
    r = np.arange(c)
    tri = (r[None, :] >= r[:, None]) if reverse else (r[None, :] <= r[:, None])
    sel = (np.arange(sub * HEAD_DIM)[:, None] // HEAD_DIM) == np.arange(HEAD_DIM)[None, :]
    in_specs = [zspec(0), zspec((1 + d) * hb), zspec(3 * hb),
                pl.BlockSpec((1, 1, w), lambda h, i: (d * hb + h, 0, 0)),
                pl.BlockSpec((c, c), lambda h, i: (0, 0)),
                pl.BlockSpec((sub * HEAD_DIM, HEAD_DIM), lambda h, i: (0, 0))]
    args = [z, z, z, lb.reshape(2 * hb, 1, w), jnp.asarray(tri, BF16), jnp.asarray(sel, BF16)]
    if final:
        in_specs += [ospec, zspec(4 * hb)]
        args += [o_fwd, z]
    return pl.pallas_call(
        functools.partial(_hg_kernel, reverse=reverse, final=final, chunk=c, sub=sub, heads=heads),
        grid=(hb, n_tot),
        in_specs=in_specs,
        out_specs=ospec,
        out_shape=jax.ShapeDtypeStruct((l, HG_W), BF16 if final else F32),
        scratch_shapes=[pltpu.VMEM((heads, HEAD_DIM, HEAD_DIM), F32), pltpu.VMEM((heads, c, HEAD_DIM), F32),
                        pltpu.VMEM((heads, c, HEAD_DIM), F32), pltpu.VMEM((heads, c, sub * HEAD_DIM), BF16)],
        compiler_params=_cparams(("arbitrary", "arbitrary")),
        name="hgrn2_bwd" if reverse else "hgrn2_fwd",
    )(*args)


def _wout_kernel(oret_ref, ohy_ref, ohg_ref, w_ref, x_ref, g_ref, mod_ref, xo_ref, h_ref, *, tm, ctx_len):
    i = pl.program_id(0)
    y = (_dot(oret_ref[...], w_ref[0:RET_W, :]) + _dot(ohy_ref[...].astype(BF16), w_ref[RET_W:RET_W + HY_W, :])
         + _dot(ohg_ref[...], w_ref[RET_W + HY_W:, :]))
    row = i * tm + lax.broadcasted_iota(jnp.int32, (tm, 1), 0)
    gate = jnp.where(row < ctx_len, mod_ref[0:1, :], mod_ref[1:2, :])
    x = x_ref[...] + gate * y
    xo_ref[...] = x
    h_ref[...] = _norm_mod(x, g_ref[...], mod_ref, i * tm, ctx_len, 2)


def out_proj(o_ret, o_hy, o_hg, w_bf16, x, gain, mod, ctx_len, tm):
    l, d = x.shape
    return pl.pallas_call(
        functools.partial(_wout_kernel, tm=tm, ctx_len=ctx_len),
        grid=(l // tm,),
        in_specs=[pl.BlockSpec((tm, RET_W), lambda i: (i, 0)),
                  pl.BlockSpec((tm, HY_W), lambda i: (i, 0)),
                  pl.BlockSpec((tm, HG_W), lambda i: (i, 0)),
                  pl.BlockSpec((d, d), lambda i: (0, 0)),
                  pl.BlockSpec((tm, d), lambda i: (i, 0)),
                  pl.BlockSpec((1, d), lambda i: (0, 0)),
                  pl.BlockSpec((8, d), lambda i: (0, 0))],
        out_specs=[pl.BlockSpec((tm, d), lambda i: (i, 0)), pl.BlockSpec((tm, d), lambda i: (i, 0))],
        out_shape=[jax.ShapeDtypeStruct((l, d), F32), jax.ShapeDtypeStruct((l, d), F32)],
        compiler_params=_cparams(("arbitrary",)),
        name="out_proj",
    )(o_ret, o_hy, o_hg, w_bf16, x, gain.reshape(1, d), mod)


def _router_kernel(h_ref, w_ref, b_ref, idx_ref, wt_ref):
    logits = _dot(h_ref[...].astype(BF16), w_ref[...]) + b_ref[...]
    lane = lax.broadcasted_iota(jnp.int32, logits.shape, 1).astype(F32)
    vals, idxs = [], []
    x = logits
    for _ in range(TOP_K):
        mx = jnp.max(x, axis=-1, keepdims=True)
        ix = jnp.min(jnp.where(x == mx, lane, float(LANES)), axis=-1, keepdims=True)
        vals.append(mx)
        idxs.append(ix)
        x = jnp.where(lane == ix, -jnp.inf, x)
    es = [jnp.exp(vv - vals[0]) for vv in vals]
    tot = es[0] + es[1] + es[2] + es[3]
    idx_out = jnp.zeros(logits.shape, F32)
    wt_out = jnp.zeros(logits.shape, F32)
    for r in range(TOP_K):
        idx_out = jnp.where(lane == float(r), idxs[r], idx_out)
        wt_out = jnp.where(lane == float(r), es[r] / tot, wt_out)
    idx_ref[...] = idx_out.astype(jnp.int32)
    wt_ref[...] = wt_out


def router(h2, router_w, router_b, row0, n_tok, tm=256):
    d = h2.shape[1]
    w = jnp.zeros((d, LANES), BF16).at[:, :N_EXPERTS].set(router_w.astype(BF16))
    b = jnp.full((1, LANES), -jnp.inf, F32).at[0, :N_EXPERTS].set(router_b)
    return pl.pallas_call(
        _router_kernel,
        grid=(n_tok // tm,),
        in_specs=[pl.BlockSpec((tm, d), lambda i: (i + row0 // tm, 0)),
                  pl.BlockSpec((d, LANES), lambda i: (0, 0)),
                  pl.BlockSpec((1, LANES), lambda i: (0, 0))],
        out_specs=[pl.BlockSpec((tm, LANES), lambda i: (i, 0)), pl.BlockSpec((tm, LANES), lambda i: (i, 0))],
        out_shape=[jax.ShapeDtypeStruct((n_tok, LANES), jnp.int32), jax.ShapeDtypeStruct((n_tok, LANES), F32)],
        compiler_params=_cparams(("arbitrary",)),
        name="router",
    )(h2, w, b)


def _row_copy(src_hbm, src_row, dst, dst_row, sem):
    return pltpu.make_async_copy(src_hbm.at[pl.ds(src_row, 1)], dst.at[pl.ds(dst_row, 1)], sem)


def _group_wait(src_hbm, dst, dst_row0, sem):
    pltpu.make_async_copy(src_hbm.at[pl.ds(0, GATHER_UNROLL)], dst.at[pl.ds(dst_row0, GATHER_UNROLL)], sem).wait()


def _expert_kernel(blk_e_ref, nused_ref, valid_ref, tok_ref, h_hbm, wg_ref, bg_ref, wu_ref, bu_ref, wd_ref, bd_ref,
                   o_ref, xf_scr, xb_scr, sem, *, row0):
    b = pl.program_id(0)
    f = pl.program_id(1)
    nused = nused_ref[0]

    def gather(blk, wait):
        def body(g, carry):
            r0 = pl.multiple_of(g * GATHER_UNROLL, GATHER_UNROLL)
            if wait:
                _group_wait(h_hbm, xf_scr, r0, sem)
            else:
                for u in range(GATHER_UNROLL):
                    _row_copy(h_hbm, row0 + tok_ref[blk * MOE_TM + r0 + u], xf_scr, r0 + u, sem).start()
            return carry

        n_groups = lax.shift_right_logical(valid_ref[blk] + (GATHER_UNROLL - 1), GATHER_UNROLL.bit_length() - 1)
        lax.fori_loop(0, n_groups, body, 0)

    @pl.when(b < nused)
    def _():
        @pl.when(f == 0)
        def _():
            @pl.when(b == 0)
            def _():
                xf_scr[...] = jnp.zeros_like(xf_scr)
                gather(0, False)

            gather(b, True)
            xb_scr[...] = xf_scr[...].astype(BF16)

        @pl.when((f == 1) & (b + 1 < nused))
        def _():
            gather(b + 1, False)

        x = xb_scr[...]
        gate = jnp.minimum(_dot(x, wg_ref[0].astype(BF16)) + bg_ref[0], SWIGLU_LIMIT)
        up = jnp.clip(_dot(x, wu_ref[0].astype(BF16)) + bu_ref[0], -SWIGLU_LIMIT, SWIGLU_LIMIT)
        act = gate * jax.nn.sigmoid(SWIGLU_ALPHA * gate) * (up + 1.0)
        part = _dot(act.astype(BF16), wd_ref[0].astype(BF16))

        @pl.when(f == 0)
        def _():
            o_ref[...] = part + bd_ref[0]

        @pl.when(f != 0)
        def _():
            o_ref[...] += part

    @pl.when((b >= nused) & (f == 0))
    def _():
        o_ref[...] = jnp.zeros_like(o_ref)


def expert_ffn(h2, row0, slot_tok, blk_e, n_used, blk_valid, layer, w_gate, b_gate, w_up, b_up, w_down, b_down):
    d = h2.shape[1]
    n_rows = slot_tok.shape[0]
    depth, e, _, dff = w_gate.shape
    nb, nf = n_rows // MOE_TM, dff // MOE_TF
    assert nf >= 2 and MOE_TM % GATHER_UNROLL == 0

    def blk(b, nu):
        return jnp.minimum(b, nu[0] - 1)

    def fidx(b, f, nu):
        return jnp.where(b < nu[0], f, nf - 1)

    grid_spec = pltpu.PrefetchScalarGridSpec(
        num_scalar_prefetch=4,
        grid=(nb, nf),
        in_specs=[pl.BlockSpec(memory_space=pl.ANY),
                  pl.BlockSpec((None, 1, d, MOE_TF), lambda b, f, be, nu, vl, tk: (layer, be[blk(b, nu)], 0, fidx(b, f, nu))),
                  pl.BlockSpec((None, 1, 1, MOE_TF), lambda b, f, be, nu, vl, tk: (layer, be[blk(b, nu)], 0, fidx(b, f, nu))),
                  pl.BlockSpec((None, 1, d, MOE_TF), lambda b, f, be, nu, vl, tk: (layer, be[blk(b, nu)], 0, fidx(b, f, nu))),
                  pl.BlockSpec((None, 1, 1, MOE_TF), lambda b, f, be, nu, vl, tk: (layer, be[blk(b, nu)], 0, fidx(b, f, nu))),
                  pl.BlockSpec((None, 1, MOE_TF, d), lambda b, f, be, nu, vl, tk: (layer, be[blk(b, nu)], fidx(b, f, nu), 0)),
                  pl.BlockSpec((None, 1, 1, d), lambda b, f, be, nu, vl, tk: (layer, be[blk(b, nu)], 0, 0))],
        out_specs=pl.BlockSpec((MOE_TM, d), lambda b, f, be, nu, vl, tk: (b, 0)),
        scratch_shapes=[pltpu.VMEM((MOE_TM, d), F32), pltpu.VMEM((MOE_TM, d), BF16), pltpu.SemaphoreType.DMA],
    )
    return pl.pallas_call(
        functools.partial(_expert_kernel, row0=row0),
        grid_spec=grid_spec,
        out_shape=jax.ShapeDtypeStruct((n_rows, d), F32),
        compiler_params=_cparams(("arbitrary", "arbitrary")),
        name="expert_ffn",
    )(blk_e, n_used, blk_valid, slot_tok, h2, w_gate, b_gate.reshape(depth, e, 1, dff), w_up,
      b_up.reshape(depth, e, 1, dff), w_down, b_down.reshape(depth, e, 1, d))


def _combine_kernel(dest_ref, y_hbm, wt_ref, x_ref, mod_ref, g_ref, o_ref, y_scr, sem, *, tm, row0, ctx_len, final):
    i = pl.program_id(0)

    def gather(tile, wait):
        def body(g, carry):
            r0 = pl.multiple_of(g * GATHER_UNROLL, GATHER_UNROLL)
            for k in range(TOP_K):
                dst = y_scr.at[tile % 2, k]
                if wait:
                    _group_wait(y_hbm, dst, r0, sem.at[tile % 2])
                else:
                    for u in range(GATHER_UNROLL):
                        _row_copy(y_hbm, dest_ref[(tile * tm + r0 + u) * TOP_K + k], dst, r0 + u, sem.at[tile % 2]).start()
            return carry

        lax.fori_loop(0, tm // GATHER_UNROLL, body, 0)

    @pl.when(i == 0)
    def _():
        gather(0, False)

    @pl.when(i + 1 < pl.num_programs(0))
    def _():
        gather(i + 1, False)

    gather(i, True)
    wt = wt_ref[...]
    ys = y_scr.at[i % 2]
    acc = ys[0] * wt[:, 0:1]
    for k in range(1, TOP_K):
        acc = acc + ys[k] * wt[:, k:k + 1]
    row = row0 + i * tm + lax.broadcasted_iota(jnp.int32, (tm, 1), 0)
    x = x_ref[...] + jnp.where(row < ctx_len, mod_ref[0:1, :], mod_ref[1:2, :]) * acc
    if final:
        x = x * lax.rsqrt(jnp.mean(x * x, axis=-1, keepdims=True) + EPS) * g_ref[...]
    o_ref[...] = x


def moe_combine(ys, dest, top_w, x, mod, final_gain, row0, n_tok, ctx_len, final, tm=128):
    d = x.shape[1]
    grid_spec = pltpu.PrefetchScalarGridSpec(
        num_scalar_prefetch=1,
        grid=(n_tok // tm,),
        in_specs=[pl.BlockSpec(memory_space=pl.ANY),
                  pl.BlockSpec((tm, LANES), lambda i, ds: (i, 0)),
                  pl.BlockSpec((tm, d), lambda i, ds: (i + row0 // tm, 0)),
                  pl.BlockSpec((8, d), lambda i, ds: (0, 0)),
                  pl.BlockSpec((1, d), lambda i, ds: (0, 0))],
        out_specs=pl.BlockSpec((tm, d), lambda i, ds: (i, 0)),
        scratch_shapes=[pltpu.VMEM((2, TOP_K, tm, d), F32), pltpu.SemaphoreType.DMA((2,))],
    )
    return pl.pallas_call(
        functools.partial(_combine_kernel, tm=tm, row0=row0, ctx_len=ctx_len, final=final),
        grid_spec=grid_spec,
        out_shape=jax.ShapeDtypeStruct((n_tok, d), F32),
        compiler_params=_cparams(("arbitrary",)),
        name="moe_combine",
    )(dest, ys, top_w, x, mod, final_gain.reshape(1, d))


FFT_R = 128
FFT_N = FFT_R * FFT_R
HY_TC = 128
HY_TCB = HY_TC


def _hypre_kernel(z_ref, zp_ref, zn_ref, w_ref, b_ref, oc_ref, ol_ref, *, tm, ctx_len, l_tot):
    i = pl.program_id(1)
    x = z_ref[...]
    loc = lax.broadcasted_iota(jnp.int32, (tm, 1), 0)
    row = i * tm + loc
    prev = jnp.where(loc == 0, zp_ref[7:8, :], pltpu.roll(x, 1, 0))
    prev = jnp.where((row == 0) | (row == ctx_len), 0.0, prev)
    nxt = jnp.where(loc == tm - 1, zn_ref[0:1, :], pltpu.roll(x, tm - 1, 0))
    nxt = jnp.where((row == ctx_len - 1) | (row == l_tot - 1), 0.0, nxt)
    y = w_ref[0:1, :] * prev + w_ref[1:2, :] * x + w_ref[2:3, :] * nxt + b_ref[...]

    @pl.when(i == 0)
    def _():
        oc_ref[...] = y

    @pl.when(i > 0)
    def _():
        ol_ref[...] = y


def hyena_pre(z, conv_w, conv_b, ctx_len):
    l = z.shape[0]
    tm, tc = ctx_len, 512
    cb0 = RET_COLS // tc
    nr8 = l // 8
    return pl.pallas_call(
        functools.partial(_hypre_kernel, tm=tm, ctx_len=ctx_len, l_tot=l),
        grid=(HY_COLS // tc, l // tm),
        in_specs=[pl.BlockSpec((tm, tc), lambda c, i: (i, cb0 + c)),
                  pl.BlockSpec((8, tc), lambda c, i: (jnp.maximum(i * (tm // 8) - 1, 0), cb0 + c)),
                  pl.BlockSpec((8, tc), lambda c, i: (jnp.minimum((i + 1) * (tm // 8), nr8 - 1), cb0 + c)),
                  pl.BlockSpec((3, tc), lambda c, i: (0, c)),
                  pl.BlockSpec((1, tc), lambda c, i: (0, c))],
        out_specs=[pl.BlockSpec((tm, tc), lambda c, i: (0, c)),
                   pl.BlockSpec((tm, tc), lambda c, i: (jnp.maximum(i - 1, 0), c))],
        out_shape=[jax.ShapeDtypeStruct((ctx_len, HY_COLS), F32), jax.ShapeDtypeStruct((l - ctx_len, HY_COLS), F32)],
        compiler_params=_cparams(("arbitrary", "arbitrary")),
        name="hyena_pre",
    )(z, z, z, conv_w, conv_b.reshape(1, HY_COLS))


def _hyfilt_kernel(ft_ref, aux_ref, w1_ref, b1_ref, fr_ref, w2_ref, b2_ref, w3_ref, dl_ref, o_ref):
    fr = fr_ref[...]
    h = jnp.sin(fr * (_dot(ft_ref[...].astype(BF16), w1_ref[...].astype(BF16)) + b1_ref[...]))
    h = jnp.sin(fr * (_dot(h.astype(BF16), w2_ref[...].astype(BF16)) + b2_ref[...]))
    h = _dot(h.astype(BF16), w3_ref[...].astype(BF16))
    aux = aux_ref[...]
    t, m_f, m_b = aux[:, 0:1], aux[:, 1:2], aux[:, 2:3]
    window = jnp.exp(-t * dl_ref[...])
    for o in range(2):
        base = o * 2 * HY_W
        o_ref[:, o * HY_W:(o + 1) * HY_W] = (m_f * h[:, base:base + HY_W] + m_b * h[:, base + HY_W:base + 2 * HY_W]) * window


def hyena_filters(seq_len, circle, w1, b1, freq, w2, b2, w3):
    L = seq_len
    if circle:
        r = np.arange(FFT_N)
        is_f, is_b = r < L, r > FFT_N - L
        pos = np.where(is_f, r, np.where(is_b, FFT_N - r, 0))
    else:
        lag = np.arange(2 * L) - L
        is_f, is_b = lag >= 0, (lag < 0) & (lag > -L)
        pos = np.where(is_b, -lag, np.where(is_f, lag, 0))
    n_rows = pos.shape[0]
    t_lin = np.linspace(0.0, 1.0, L, dtype=np.float32)[:, None]
    w_ang = np.float32(2.0 * math.pi) * np.arange(L, dtype=np.float32)[:, None] / np.float32(L)
    f = np.linspace(1e-4, HY_BANDS - 1, HY_BANDS, dtype=np.float32)[None, :]
    feats = np.concatenate([t_lin, np.cos(f * w_ang), -np.sin(f * w_ang)], axis=-1)[pos]
    ne, nw = feats.shape[1], w1.shape[1]
    feats = np.pad(feats, ((0, 0), (0, LANES - ne)))
    aux = np.zeros((n_rows, LANES), np.float32)
    aux[:, 0], aux[:, 1], aux[:, 2] = t_lin[pos, 0], is_f, is_b
    w1p = jnp.pad(w1, ((0, LANES - ne), (0, LANES - nw)))
    w2p = jnp.pad(w2, ((0, LANES - nw), (0, LANES - nw)))
    w3p = jnp.pad(w3, ((0, LANES - nw), (0, 0)))
    pad1 = lambda a: jnp.pad(a, (0, LANES - nw)).reshape(1, LANES)
    deltas = np.abs(np.linspace(HY_MIN_DECAY, HY_MAX_DECAY, HY_W, dtype=np.float32)).reshape(1, HY_W)
    tm = min(1024, n_rows)
    full = lambda shape: pl.BlockSpec(shape, lambda i: (0,) * len(shape))
    return pl.pallas_call(
        _hyfilt_kernel,
        grid=(n_rows // tm,),
        in_specs=[pl.BlockSpec((tm, LANES), lambda i: (i, 0)), pl.BlockSpec((tm, LANES), lambda i: (i, 0)),
                  full((LANES, LANES)), full((1, LANES)), full((1, LANES)), full((LANES, LANES)), full((1, LANES)),
                  full((LANES, 4 * HY_W)), full((1, HY_W))],
        out_specs=pl.BlockSpec((tm, 2 * HY_W), lambda i: (i, 0)),
        out_shape=jax.ShapeDtypeStruct((n_rows, 2 * HY_W), F32),
        compiler_params=_cparams(("arbitrary",)),
        name="hyena_filters",
    )(jnp.asarray(feats), jnp.asarray(aux), w1p, pad1(b1), pad1(freq), w2p, pad1(b2), w3p, jnp.asarray(deltas))


def _dft_consts():
    k = np.arange(FFT_R)
    ang = 2.0 * np.pi * np.outer(k, k) / FFT_R
    c, s = np.cos(ang), np.sin(ang)
    fwd_a = np.concatenate([c, -s], axis=0)
    fwd_b = np.block([[c, s], [-s, c]])
    inv_b = np.block([[c, -s], [s, c]])
    inv_a = np.concatenate([c, -s], axis=1) / FFT_N
    tw = 2.0 * np.pi * np.outer(k, k) / FFT_N
    as32 = lambda a: jnp.asarray(a, F32)
    return (as32(fwd_a), as32(fwd_b), as32(inv_b), as32(inv_a),
            as32(np.cos(tw))[:, :, None], as32(-np.sin(tw))[:, :, None])


def _store_grouped(o_ref, idx, res):
    for part in range(2):
        for g in range(FFT_R // 8):
            o_ref[g, idx, part] = res[part * FFT_R + g * 8:part * FFT_R + (g + 1) * 8, :]


def _fft_a_kernel(x_ref, f_ref, o_ref, *, n1, natural):
    f = f_ref[...].astype(BF16)
    for n2 in range(FFT_R):
        xs = x_ref[pl.ds(n2, n1, stride=FFT_R), :] if natural else x_ref[n2]
        _store_grouped(o_ref, n2, _dot(f, xs.astype(BF16)))


def fft_stage_a(x, n1, natural, c, col0=0):
    fwd_a = _dft_consts()[0][:, :n1]
    xspec = (pl.BlockSpec((n1 * FFT_R, HY_TC), lambda j: (0, col0 + j)) if natural
             else pl.BlockSpec((FFT_R, n1, HY_TC), lambda j: (0, 0, col0 + j)))
    out = pl.pallas_call(
        functools.partial(_fft_a_kernel, n1=n1, natural=natural),
        grid=(c // HY_TC,),
        in_specs=[xspec, pl.BlockSpec((2 * FFT_R, n1), lambda j: (0, 0))],
        out_specs=pl.BlockSpec((FFT_R // 8, FFT_R, 2, 8, HY_TC), lambda j: (0, 0, 0, 0, j)),
        out_shape=jax.ShapeDtypeStruct((FFT_R // 8, FFT_R, 2, 8, c), F32),
        compiler_params=_cparams(("arbitrary",)),
        name="fft_stage_a",
    )(x, fwd_a)
    return out.reshape(FFT_R // 8, FFT_R * 16, c)


def _load_pair(ref, j):
    return jnp.concatenate([ref[pl.ds(j, FFT_R, stride=16), :], ref[pl.ds(8 + j, FFT_R, stride=16), :]], axis=0)


def _twiddle(v, tr, ti):
    vr, vi = v[:FFT_R], v[FFT_R:]
    return jnp.concatenate([vr * tr - vi * ti, vr * ti + vi * tr], axis=0)


def _fft_b_spec_kernel(a_ref, twr_ref, twi_ref, fb_ref, o_ref):
    fb = fb_ref[...].astype(BF16)
    for kk in range(8):
        p = _twiddle(_load_pair(a_ref, kk), twr_ref[kk], twi_ref[kk])
        o_ref[kk] = _dot(fb, p.astype(BF16))


def _fft_b_conv_kernel(a_ref, h_ref, twr_ref, twi_ref, fb_ref, fib_ref, o_ref):
    fb = fb_ref[...].astype(BF16)
    fib = fib_ref[...].astype(BF16)
    for kk in range(8):
        tr, ti = twr_ref[kk], twi_ref[kk]
        x = _dot(fb, _twiddle(_load_pair(a_ref, kk), tr, ti).astype(BF16))
        h = h_ref[kk]
        y = _twiddle(x, h[:FFT_R], h[FFT_R:])
        z = _twiddle(_dot(fib, y.astype(BF16)), tr, -ti)
        for part in range(2):
            for g in range(FFT_R // 8):
                o_ref[g, kk, part] = z[part * FFT_R + g * 8:part * FFT_R + (g + 1) * 8, :]


def _bspecs(c):
    del c
    return [pl.BlockSpec((None, FFT_R * 16, HY_TCB), lambda j, g: (g, 0, j))]


def fft_stage_b_spec(a):
    c = a.shape[-1]
    _, fwd_b, _, _, twr, twi = _dft_consts()
    tws = pl.BlockSpec((8, FFT_R, 1), lambda j, g: (g, 0, 0))
    return pl.pallas_call(
        _fft_b_spec_kernel,
        grid=(c // HY_TCB, FFT_R // 8),
        in_specs=_bspecs(c) + [tws, tws, pl.BlockSpec((2 * FFT_R, 2 * FFT_R), lambda j, g: (0, 0))],
        out_specs=pl.BlockSpec((8, 2 * FFT_R, HY_TCB), lambda j, g: (g, 0, j)),
        out_shape=jax.ShapeDtypeStruct((FFT_R, 2 * FFT_R, c), F32),
        compiler_params=_cparams(("arbitrary", "arbitrary")),
        name="fft_stage_b_spec",
    )(a, twr, twi, fwd_b)


def fft_stage_b_conv(a, hspec, h_col0):
    c = a.shape[-1]
    _, fwd_b, inv_b, _, twr, twi = _dft_consts()
    tws = pl.BlockSpec((8, FFT_R, 1), lambda j, g: (g, 0, 0))
    sq = pl.BlockSpec((2 * FFT_R, 2 * FFT_R), lambda j, g: (0, 0))
    out = pl.pallas_call(
        _fft_b_conv_kernel,
        grid=(c // HY_TCB, FFT_R // 8),
        in_specs=_bspecs(c) + [pl.BlockSpec((8, 2 * FFT_R, HY_TCB), lambda j, g: (g, 0, h_col0 + j)), tws, tws, sq, sq],
        out_specs=pl.BlockSpec((FFT_R // 8, 8, 2, 8, HY_TCB), lambda j, g: (0, g, 0, 0, j)),
        out_shape=jax.ShapeDtypeStruct((FFT_R // 8, FFT_R, 2, 8, c), F32),
        compiler_params=_cparams(("arbitrary", "arbitrary")),
        name="fft_stage_b_conv",
    )(a, hspec, twr, twi, fwd_b, inv_b)
    return out.reshape(FFT_R // 8, FFT_R * 16, c)


def _fft_ainv_kernel(z_ref, u_ref, gate_ref, skip_ref, fi_ref, o_ref, *, n1, u_natural):
    g2 = pl.program_id(1)
    fi = fi_ref[...].astype(BF16)
    span = (n1 - 1) * FFT_R + 8
    base = pl.multiple_of(g2 * 8, 8)
    gate_rows = gate_ref.at[pl.ds(base, span)]
    u_rows = u_ref.at[pl.ds(base, span)] if u_natural else None
    for jj in range(8):
        y = _dot(fi, _load_pair(z_ref, jj).astype(BF16))
        u = u_rows[pl.ds(jj, n1, stride=FFT_R), :] if u_natural else u_ref[jj]
        o_ref[jj] = gate_rows[pl.ds(jj, n1, stride=FFT_R), :] * (y + skip_ref[...] * u)


def fft_stage_a_inv(z, u, gate, skip, gate_col, n1, u_natural):
    c = z.shape[-1]
    inv_a = _dft_consts()[3][:n1]
    ctiles = c // HY_TCB
    uspec = (pl.BlockSpec((n1 * FFT_R, HY_TCB), lambda j, g: (0, j)) if u_natural
             else pl.BlockSpec((8, n1, HY_TCB), lambda j, g: (g, 0, j)))
    return pl.pallas_call(
        functools.partial(_fft_ainv_kernel, n1=n1, u_natural=u_natural),
        grid=(ctiles, FFT_R // 8),
        in_specs=[pl.BlockSpec((None, FFT_R * 16, HY_TCB), lambda j, g: (g, 0, j)),
                  uspec,
                  pl.BlockSpec((n1 * FFT_R, HY_TCB), lambda j, g: (0, gate_col * ctiles + j)),
                  pl.BlockSpec((1, HY_TCB), lambda j, g: (0, j)),
                  pl.BlockSpec((n1, 2 * FFT_R), lambda j, g: (0, 0))],
        out_specs=pl.BlockSpec((8, n1, HY_TCB), lambda j, g: (g, 0, j)),
        out_shape=jax.ShapeDtypeStruct((FFT_R, n1, c), F32),
        compiler_params=_cparams(("arbitrary", "arbitrary")),
        name="fft_stage_a_inv",
    )(z, u, gate, skip.reshape(1, c), inv_a)


def hyena_long(u3, filt, skip):
    n1 = u3.shape[0] // FFT_R
    ct = HY_W // HY_TCB
    hspec = fft_stage_b_spec(fft_stage_a(filt, FFT_R, True, 2 * HY_W))
    a = fft_stage_a(u3, n1, True, HY_W)
    y1 = fft_stage_a_inv(fft_stage_b_conv(a, hspec, 0), u3, u3, skip[0], 1, n1, True)
    a = fft_stage_a(y1, n1, False, HY_W)
    return fft_stage_a_inv(fft_stage_b_conv(a, hspec, ct), y1, u3, skip[1], 2, n1, False)


def _hyctx_kernel(u_ref, h_ref, skip_ref, o_ref, vs_scr, *, seq_len):
    L = seq_len
    zeros = jnp.zeros((L, HY_TC), F32)

    def conv(u, order):
        padded = jnp.concatenate([zeros, u, zeros], axis=0)
        for b in range(8):
            vs_scr[b] = padded if b == 0 else pltpu.roll(padded, b, 0)

        def body(a, acc):
            taps = h_ref[pl.ds(pl.multiple_of(8 * a, 8), 8), order * HY_TC:(order + 1) * HY_TC]
            start = pl.multiple_of(2 * L - 8 * a, 8)
            for b in range(8):
                acc = acc + taps[b:b + 1, :] * vs_scr[b, pl.ds(start, L), :]
            return acc

        return lax.fori_loop(0, 2 * L // 8, body, zeros)

    v, x1, x2 = u_ref[:, 0:HY_TC], u_ref[:, HY_TC:2 * HY_TC], u_ref[:, 2 * HY_TC:3 * HY_TC]
    y = x1 * (conv(v, 0) + skip_ref[0:1, :] * v)
    o_ref[...] = x2 * (conv(y, 1) + skip_ref[1:2, :] * y)


def hyena_short(u3, lag_filt, skip):
    L = u3.shape[0]
    ct = HY_W // HY_TC
    u_t = u3.reshape(L, 3, ct, HY_TC).transpose(0, 2, 1, 3).reshape(L, ct * 3 * HY_TC)
    h_t = lag_filt.reshape(2 * L, 2, ct, HY_TC).transpose(0, 2, 1, 3).reshape(2 * L, ct * 2 * HY_TC)
    return pl.pallas_call(
        functools.partial(_hyctx_kernel, seq_len=L),
        grid=(ct,),
        in_specs=[pl.BlockSpec((L, 3 * HY_TC), lambda j: (0, j)),
                  pl.BlockSpec((2 * L, 2 * HY_TC), lambda j: (0, j)),
                  pl.BlockSpec((2, HY_TC), lambda j: (0, j))],
        out_specs=pl.BlockSpec((L, HY_TC), lambda j: (0, j)),
        out_shape=jax.ShapeDtypeStruct((L, HY_W), F32),
        scratch_shapes=[pltpu.VMEM((8, 3 * L, HY_TC), F32)],
        compiler_params=_cparams(("arbitrary",)),
        name="hyena_short",
    )(u_t, h_t, skip)


def _unpermute(y):
    return jnp.swapaxes(y, 0, 1).reshape(-1, y.shape[-1])


def _rope_tables(ctx_len, n_rows):
    nf = HEAD_DIM // 4
    inv = ROPE_BASE ** (-jnp.arange(nf, dtype=F32) / nf)
    rows = jnp.repeat(jnp.arange(n_rows, dtype=F32), GRID_W)
    cols = (jnp.arange(n_rows * GRID_W) % GRID_W).astype(F32)
    ang = jnp.concatenate([rows[:, None] * inv, cols[:, None] * inv], axis=-1)
    ang = jnp.concatenate([jnp.zeros((ctx_len, HEAD_DIM // 2), F32), ang], axis=0)
    cos_t, sin_t = jnp.cos(ang), jnp.sin(ang)
    return jnp.concatenate([cos_t, cos_t], axis=-1), jnp.concatenate([-sin_t, sin_t], axis=-1)


def _moe(h2, x, mod_gate, final_gain, layer, moe_p, row0, n_tok, ctx_len, final):
    router_w, router_b, w_gate, b_gate, w_up, b_up, w_down, b_down = moe_p
    router_w, router_b = router_w[layer], router_b[layer]
    top_idx_p, top_w_p = router(h2, router_w, router_b, row0, n_tok)
    top_idx = top_idx_p[:, :TOP_K]
    onehot = (top_idx[:, :, None] == jnp.arange(N_EXPERTS)[None, None, :]).astype(jnp.int32)
    per_tok = onehot.sum(axis=1)
    before = jnp.cumsum(per_tok, axis=0) - per_tok
    counts = per_tok.sum(axis=0)
    nblk = (counts + MOE_TM - 1) // MOE_TM
    blk_end = jnp.cumsum(nblk)
    pad_start = (blk_end - nblk) * MOE_TM
    dest = (pad_start[top_idx] + jnp.take_along_axis(before, top_idx, axis=1)).astype(jnp.int32)
    n_blocks = -(-(n_tok * TOP_K + N_EXPERTS * (MOE_TM - 1)) // MOE_TM)
    blk_e = jnp.minimum(jnp.searchsorted(blk_end, jnp.arange(n_blocks), side='right'), N_EXPERTS - 1)
    n_used = blk_end[-1:].astype(jnp.int32)
    blk_first = (blk_end - nblk)[blk_e]
    blk_valid = jnp.clip(counts[blk_e] - (jnp.arange(n_blocks) - blk_first) * MOE_TM, 0, MOE_TM).astype(jnp.int32)
    tok = jnp.broadcast_to(jnp.arange(n_tok, dtype=jnp.int32)[:, None], dest.shape)
    slot_tok = jnp.zeros((n_blocks * MOE_TM,), jnp.int32).at[dest.reshape(-1)].set(tok.reshape(-1))
    ys = expert_ffn(h2, row0, slot_tok, blk_e.astype(jnp.int32), n_used, blk_valid, layer, w_gate, b_gate, w_up, b_up,
                    w_down, b_down)
    return moe_combine(ys, dest.reshape(-1), top_w_p, x, mod_gate, final_gain, row0, n_tok, ctx_len, final)


def kernel(x, c, ctx, c_ctx, norm_mix_g, norm_ffn_g, final_norm_g, ada_w, ada_b, w_in, w_out, ret_decay_logit,
           hy_conv_w, hy_conv_b, hy_filt_w1, hy_filt_b1, hy_filt_freq, hy_filt_w2, hy_filt_b2, hy_filt_w3, hy_skip,
           hg_lb_logits, router_w, router_b, moe_w_gate, moe_b_gate, moe_w_up, moe_b_up, moe_w_down, moe_b_down):
    bsz, seq, d = x.shape
    assert bsz == 1
    ctx_len = ctx.shape[1]
    l = ctx_len + seq
    cos2, sin2 = _rope_tables(ctx_len, seq // GRID_W)
    cvec = jnp.zeros((8, d), F32).at[0].set(c_ctx).at[1].set(c[0])
    mod = adaln(cvec, ada_w, ada_b)
    xs = jnp.concatenate([ctx[0], x[0]], axis=0)
    p_lb = jax.nn.softmax(hg_lb_logits.astype(F32), axis=0)
    lb_cum = jnp.cumsum(p_lb, axis=0)
    for layer in range(DEPTH):
        last = layer == DEPTH - 1
        m = mod[layer].reshape(8, N_MOD, d)
        mc, ml = m[0], m[1]
        mod_in = jnp.stack([mc[0], mc[1], ml[0], ml[1], mc[0], mc[0], mc[0], mc[0]])
        z = in_proj(xs, norm_mix_g[layer], mod_in, w_in[layer].astype(BF16), ctx_len, tm=768, tn=1408)
        log_gamma = jax.nn.log_sigmoid(ret_decay_logit[layer].astype(F32))
        o_ret = retention_dir(z, log_gamma, cos2, sin2, None, ctx_len, reverse=False)
        o_ret = retention_dir(z, log_gamma, cos2, sin2, o_ret, ctx_len, reverse=True)
        lb = lb_cum[layer] - p_lb[0]
        o_hg = hgrn2_dir(z, lb, None, ctx_len, reverse=False)
        o_hg = hgrn2_dir(z, lb, o_hg, ctx_len, reverse=True)
        filt_p = (hy_filt_w1[layer], hy_filt_b1[layer], hy_filt_freq[layer], hy_filt_w2[layer], hy_filt_b2[layer],
                  hy_filt_w3[layer])
        u_ctx, u_lat = hyena_pre(z, hy_conv_w[layer], hy_conv_b[layer], ctx_len)
        o_hy_l = _unpermute(hyena_long(u_lat, hyena_filters(seq, True, *filt_p), hy_skip[layer]))
        if last:
            o_hy_c = jnp.zeros((ctx_len, HY_W), F32)
        else:
            o_hy_c = hyena_short(u_ctx, hyena_filters(ctx_len, False, *filt_p), hy_skip[layer])
        o_hy = jnp.concatenate([o_hy_c, o_hy_l], axis=0)
        mod_out = jnp.stack([mc[2], ml[2], mc[3], mc[4], ml[3], ml[4], mc[0], mc[0]])
        xs, h2 = out_proj(o_ret, o_hy, o_hg, w_out[layer].astype(BF16), xs, norm_ffn_g[layer], mod_out, ctx_len,
                          tm=384)
        moe_p = (router_w, router_b, moe_w_gate, moe_b_gate, moe_w_up, moe_b_up, moe_w_down, moe_b_down)
        mod_gate = jnp.stack([mc[5], ml[5], mc[0], mc[0], mc[0], mc[0], mc[0], mc[0]])
        if last:
            return _moe(h2, xs, mod_gate, final_norm_g, layer, moe_p, ctx_len, seq, ctx_len, True)[None]
        xs = _moe(h2, xs, mod_gate, final_norm_g, layer, moe_p, 0, l, ctx_len, False)
```

```python
import functools
import math

import jax
import jax.numpy as jnp
import numpy as np
from jax import lax
from jax.experimental import pallas as pl
from jax.experimental.pallas import tpu as pltpu

F32 = jnp.float32
BF16 = jnp.bfloat16

D_MODEL = 2048
DEPTH = 2
GRID_W = 64
HEAD_DIM = 128
RET_W = 3 * D_MODEL // 8
RET_H = RET_W // HEAD_DIM
HY_W = D_MODEL // 4
HG_W = D_MODEL - RET_W - HY_W
HG_H = HG_W // HEAD_DIM
RET_COLS = 4 * RET_W
HY_COLS = 3 * HY_W
HG_COLS = 5 * HG_W
IN_W = RET_COLS + HY_COLS + HG_COLS
ROPE_BASE = 10000.0
EPS = 1e-6
N_MOD = 6
HY_BANDS = 16
HY_MAX_DECAY = math.log(1e-2) / 0.3
HY_MIN_DECAY = math.log(1e-2) / 1.5
N_EXPERTS = 32
TOP_K = 4
SWIGLU_LIMIT = 7.0
SWIGLU_ALPHA = 1.702

LANES = 128
VMEM_LIMIT = 56 * 1024 * 1024

RET_CHUNK = 256
RET_HEADS_PER_STEP = 6
HG_CHUNK = 128
HG_SUB = 16
HG_HEADS_PER_STEP = 6
MOE_TM = 576
GATHER_UNROLL = 8
MOE_TF = 512


def _cparams(sem):
    return pltpu.CompilerParams(dimension_semantics=sem, vmem_limit_bytes=VMEM_LIMIT)


def _dot(a, b):
    return jnp.dot(a, b, preferred_element_type=F32)


def _dot_nt(a, b):
    return lax.dot_general(a, b, (((1,), (1,)), ((), ())), preferred_element_type=F32)


def _adaln_kernel(c_ref, w_ref, b_ref, o_ref):
    c = c_ref[...]
    cond = c * jax.nn.sigmoid(c)
    o_ref[0] = _dot(cond.astype(BF16), w_ref[0].astype(BF16)) + b_ref[0]


def adaln(cvec, ada_w, ada_b, tn=1536):
    depth, d, n = ada_w.shape
    return pl.pallas_call(
        _adaln_kernel,
        grid=(depth, n // tn),
        in_specs=[pl.BlockSpec((8, d), lambda l, j: (0, 0)),
                  pl.BlockSpec((1, d, tn), lambda l, j: (l, 0, j)),
                  pl.BlockSpec((1, 1, tn), lambda l, j: (l, 0, j))],
        out_specs=pl.BlockSpec((1, 8, tn), lambda l, j: (l, 0, j)),
        out_shape=jax.ShapeDtypeStruct((depth, 8, n), F32),
        compiler_params=_cparams(("arbitrary", "arbitrary")),
        name="adaln",
    )(cvec, ada_w, ada_b.reshape(depth, 1, n))


def _norm_mod(x, gain, mod_ref, row0, ctx_len, shift_row):
    y = x * lax.rsqrt(jnp.mean(x * x, axis=-1, keepdims=True) + EPS) * gain
    row = row0 + lax.broadcasted_iota(jnp.int32, (x.shape[0], 1), 0)
    is_ctx = row < ctx_len
    shift = jnp.where(is_ctx, mod_ref[shift_row:shift_row + 1, :], mod_ref[shift_row + 2:shift_row + 3, :])
    scale = jnp.where(is_ctx, mod_ref[shift_row + 1:shift_row + 2, :], mod_ref[shift_row + 3:shift_row + 4, :])
    return y * (1.0 + scale) + shift


def _win_kernel(x_ref, g_ref, mod_ref, w_ref, o_ref, h_scr, *, tm, ctx_len):
    i = pl.program_id(0)

    @pl.when(pl.program_id(1) == 0)
    def _():
        h_scr[...] = _norm_mod(x_ref[...], g_ref[...], mod_ref, i * tm, ctx_len, 0).astype(BF16)

    o_ref[...] = _dot(h_scr[...], w_ref[...])


def in_proj(x, gain, mod, w_bf16, ctx_len, tm, tn):
    l, d = x.shape
    n = w_bf16.shape[1]
    return pl.pallas_call(
        functools.partial(_win_kernel, tm=tm, ctx_len=ctx_len),
        grid=(l // tm, n // tn),
        in_specs=[pl.BlockSpec((tm, d), lambda i, j: (i, 0)),
                  pl.BlockSpec((1, d), lambda i, j: (0, 0)),
                  pl.BlockSpec((8, d), lambda i, j: (0, 0)),
                  pl.BlockSpec((d, tn), lambda i, j: (0, j))],
        out_specs=pl.BlockSpec((tm, tn), lambda i, j: (i, j)),
        out_shape=jax.ShapeDtypeStruct((l, n), F32),
        scratch_shapes=[pltpu.VMEM((tm, d), BF16)],
        compiler_params=_cparams(("arbitrary", "arbitrary")),
        name="in_proj",
    )(x, gain.reshape(1, d), mod, w_bf16)


def _chunk_of_step(i, n_ctx, n_tot, reverse):
    if not reverse:
        return i
    return jnp.where(i < n_ctx, n_ctx - 1 - i, n_tot - 1 - (i - n_ctx))


def _head_norm_gate(o, gate):
    return gate * (o * lax.rsqrt(jnp.mean(o * o, axis=-1, keepdims=True) + EPS))


def _ret_kernel(lg_ref, q_ref, k_ref, v_ref, cos_ref, sin_ref, *rest, reverse, final, chunk, heads):
    if final:
        of_ref, g_ref, o_ref, st_scr = rest
    else:
        o_ref, st_scr = rest
    c = chunk

    @pl.when(pl.program_id(1) == 0)
    def _():
        st_scr[...] = jnp.zeros_like(st_scr)

    cos2 = cos_ref[...]
    sin2 = sin_ref[...]

    def rope(x):
        return x * cos2 + pltpu.roll(x, HEAD_DIM // 2, 1) * sin2

    t = lax.broadcasted_iota(jnp.int32, (c, c), 0)
    s = lax.broadcasted_iota(jnp.int32, (c, c), 1)
    dist = (s - t) if reverse else (t - s)
    a = lax.broadcasted_iota(jnp.int32, (c, 1), 0)
    steps_q = ((c - a) if reverse else (a + 1)).astype(F32)
    steps_k = (a if reverse else (c - 1 - a)).astype(F32)
    for hh in range(heads):
        cols = slice(hh * HEAD_DIM, (hh + 1) * HEAD_DIM)
        lg = lg_ref[1 if reverse else 0, pl.program_id(0) * heads + hh]
        q = rope(q_ref[:, cols]).astype(BF16)
        k = rope(k_ref[:, cols]) * (HEAD_DIM ** -0.5)
        v = v_ref[:, cols]
        dec = jnp.where(dist >= 0, jnp.exp(lg * jnp.maximum(dist, 0).astype(F32)), 0.0)
        p = (_dot_nt(q, k.astype(BF16)) * dec).astype(BF16)
        st = st_scr[hh]
        o = _dot(p, v.astype(BF16)) + jnp.exp(lg * steps_q) * _dot_nt(q, st.astype(BF16))
        kd = (k * jnp.exp(lg * steps_k)).astype(BF16)
        st_scr[hh] = jnp.exp(lg * c) * st + _dot(v.T.astype(BF16), kd)
        if final:
            g = g_ref[:, cols]
            o_ref[:, cols] = _head_norm_gate(o + of_ref[:, cols], g * jax.nn.sigmoid(g)).astype(o_ref.dtype)
        else:
            o_ref[:, cols] = o


def retention_dir(z, log_gamma, cos2, sin2, o_fwd, ctx_len, reverse):
    l = z.shape[0]
    c, heads = RET_CHUNK, RET_HEADS_PER_STEP
    w = heads * HEAD_DIM
    n_tot, n_ctx = l // c, ctx_len // c
    hb = RET_W // w

    def zspec(col):
        return pl.BlockSpec((c, w), lambda h, i: (_chunk_of_step(i, n_ctx, n_tot, reverse), col + h))

    tspec = pl.BlockSpec((c, HEAD_DIM), lambda h, i: (_chunk_of_step(i, n_ctx, n_tot, reverse), 0))
    ospec = pl.BlockSpec((c, w), lambda h, i: (_chunk_of_step(i, n_ctx, n_tot, reverse), h))
    final = o_fwd is not None
    in_specs = [pl.BlockSpec(memory_space=pltpu.SMEM), zspec(0), zspec(hb), zspec(2 * hb), tspec, tspec]
    args = [log_gamma, z, z, z, cos2, sin2]
    if final:
        in_specs += [ospec, zspec(3 * hb)]
        args += [o_fwd, z]
    return pl.pallas_call(
        functools.partial(_ret_kernel, reverse=reverse, final=final, chunk=c, heads=heads),
        grid=(hb, n_tot),
        in_specs=in_specs,
        out_specs=ospec,
        out_shape=jax.ShapeDtypeStruct((l, RET_W), BF16 if final else F32),
        scratch_shapes=[pltpu.VMEM((heads, HEAD_DIM, HEAD_DIM), F32)],
        compiler_params=_cparams(("arbitrary", "arbitrary")),
        name="retention_bwd" if reverse else "retention_fwd",
    )(*args)


def _hg_kernel(q_ref, fz_ref, v_ref, lb_ref, tri_ref, sel_ref, *rest, reverse, final, chunk, sub, heads):
    if final:
        of_ref, g_ref, o_ref, st_scr, cum_scr, k_scr, p_scr = rest
    else:
        o_ref, st_scr, cum_scr, k_scr, p_scr = rest
    c = chunk

    @pl.when(pl.program_id(1) == 0)
    def _():
        st_scr[...] = jnp.zeros_like(st_scr)

    row = lax.broadcasted_iota(jnp.int32, (c, 1), 0)
    t = lax.broadcasted_iota(jnp.int32, (c, c), 0)
    s = lax.broadcasted_iota(jnp.int32, (c, c), 1)

    for hh in range(heads):
        cols = slice(hh * HEAD_DIM, (hh + 1) * HEAD_DIM)
        cum_h, k_h, p_h = cum_scr.at[hh], k_scr.at[hh], p_scr.at[hh]
        qr = q_ref[:, cols]
        q = qr * jax.nn.sigmoid(qr)
        lb = lb_ref[0][:, cols]
        sg = jax.nn.sigmoid(fz_ref[:, cols])
        logf = jnp.log(lb + (1.0 - lb) * sg)
        kk = (1.0 - lb) * (1.0 - sg)
        v32 = v_ref[:, cols]
        v = v32.astype(BF16)
        tri = tri_ref[...]
        f_hi = logf.astype(BF16)
        rest = logf - f_hi.astype(F32)
        f_mid = rest.astype(BF16)
        f_lo = (rest - f_mid.astype(F32)).astype(BF16)
        cum = _dot(tri, f_hi) + _dot(tri, f_mid) + _dot(tri, f_lo)
        cum_h[...] = cum
        k_h[...] = kk

        a_mat = jnp.zeros((c, c), F32)
        m = c // 2
        while m >= sub:
            parts = []
            for p in range(c // (2 * m)):
                r = p * 2 * m + (m if reverse else m - 1)
                parts.append(jnp.broadcast_to(cum_h[pl.ds(r, 1), :], (2 * m, HEAD_DIM)))
            dq = cum - (jnp.concatenate(parts, axis=0) if len(parts) > 1 else parts[0])
            second_half = ((row >> (m.bit_length() - 1)) & 1) == 1
            is_q = jnp.logical_not(second_half) if reverse else second_half
            eq = jnp.where(is_q, jnp.exp(jnp.minimum(dq, 0.0)), 0.0)
            ek = jnp.where(is_q, 0.0, jnp.exp(jnp.minimum(-dq, 0.0)))
            a_l = _dot_nt((q * eq).astype(BF16), (kk * ek).astype(BF16))
            a_mat = a_mat + jnp.where((t >> m.bit_length()) == (s >> m.bit_length()), a_l, 0.0)
            m //= 2

        for si in range(c):
            r0 = (si // sub) * sub
            j = si % sub
            rows = r0 + lax.broadcasted_iota(jnp.int32, (sub, 1), 0)
            keep = (rows <= si) if reverse else (rows >= si)
            e = jnp.exp(cum[r0:r0 + sub, :] - cum_h[pl.ds(si, 1), :])
            p_h[r0:r0 + sub, j * HEAD_DIM:(j + 1) * HEAD_DIM] = jnp.where(
                keep, q[r0:r0 + sub, :] * k_h[pl.ds(si, 1), :] * e, 0.0).astype(BF16)
        a_diag = _dot(p_h[...], sel_ref[...])
        pieces = []
        for b in range(c // sub):
            blk = a_diag[b * sub:(b + 1) * sub, :]
            pieces.append(blk if b == 0 else pltpu.roll(blk, b * sub, 1))
        a_mat = a_mat + jnp.concatenate(pieces, axis=0)

        st = st_scr[hh]
        o = _dot(a_mat.astype(BF16), v) + _dot_nt((q * jnp.exp(cum)).astype(BF16), st.astype(BF16))
        last = cum_h[pl.ds(0 if reverse else c - 1, 1), :]
        kd = (kk * jnp.exp(last - cum)).astype(BF16)
        st_scr[hh] = st * jnp.exp(last) + _dot(v32.T.astype(BF16), kd)
        if final:
            o_ref[:, cols] = _head_norm_gate(o + of_ref[:, cols], jax.nn.sigmoid(g_ref[:, cols])).astype(o_ref.dtype)
        else:
            o_ref[:, cols] = o


def hgrn2_dir(z, lb, o_fwd, ctx_len, reverse):
    l = z.shape[0]
    c, sub, heads = HG_CHUNK, HG_SUB, HG_HEADS_PER_STEP
    w = heads * HEAD_DIM
    n_tot, n_ctx = l // c, ctx_len // c
    col0 = (RET_COLS + HY_COLS) // w
    hb = HG_W // w
    final = o_fwd is not None
    d = 1 if reverse else 0

    def zspec(col):
        return pl.BlockSpec((c, w), lambda h, i: (_chunk_of_step(i, n_ctx, n_tot, reverse), col0 + col + h))

    ospec = pl.BlockSpec((c, w), lambda h, i: (_chunk_of_step(i, n_ctx, n_tot, reverse), h))
    r = np.arange(c)
    tri = (r[None, :] >= r[:, None]) if reverse else (r[None, :] <= r[:, None])
    sel = (np.arange(sub * HEAD_DIM)[:, None] // HEAD_DIM) == np.arange(HEAD_DIM)[None, :]
    in_specs = [zspec(0), zspec((1 + d) * hb), zspec(3 * hb),
                pl.BlockSpec((1, 1, w), lambda h, i: (d * hb + h, 0, 0)),
                pl.BlockSpec((c, c), lambda h, i: (0, 0)),
                pl.BlockSpec((sub * HEAD_DIM, HEAD_DIM), lambda h, i: (0, 0))]
    args = [z, z, z, lb.reshape(2 * hb, 1, w), jnp.asarray(tri, BF16), jnp.asarray(sel, BF16)]
    if final:
        in_specs += [ospec, zspec(4 * hb)]
        args += [o_fwd, z]
    return pl.pallas_call(
        functools.partial(_hg_kernel, reverse=reverse, final=final, chunk=c, sub=sub, heads=heads),
        grid=(hb, n_tot),
        in_specs=in_specs,
        out_specs=ospec,
        out_shape=jax.ShapeDtypeStruct((l, HG_W), BF16 if final else F32),
        scratch_shapes=[pltpu.VMEM((heads, HEAD_DIM, HEAD_DIM), F32), pltpu.VMEM((heads, c, HEAD_DIM), F32),
                        pltpu.VMEM((heads, c, HEAD_DIM), F32), pltpu.VMEM((heads, c, sub * HEAD_DIM), BF16)],
        compiler_params=_cparams(("arbitrary", "arbitrary")),
        name="hgrn2_bwd" if reverse else "hgrn2_fwd",
    )(*args)


def _wout_kernel(oret_ref, ohy_ref, ohg_ref, w_ref, x_ref, g_ref, mod_ref, xo_ref, h_ref, *, tm, ctx_len):
    i = pl.program_id(0)
    y = (_dot(oret_ref[...], w_ref[0:RET_W, :]) + _dot(ohy_ref[...].astype(BF16), w_ref[RET_W:RET_W + HY_W, :])
         + _dot(ohg_ref[...], w_ref[RET_W + HY_W:, :]))
    row = i * tm + lax.broadcasted_iota(jnp.int32, (tm, 1), 0)
    gate = jnp.where(row < ctx_len, mod_ref[0:1, :], mod_ref[1:2, :])
    x = x_ref[...] + gate * y
    xo_ref[...] = x
    h_ref[...] = _norm_mod(x, g_ref[...], mod_ref, i * tm, ctx_len, 2)


def out_proj(o_ret, o_hy, o_hg, w_bf16, x, gain, mod, ctx_len, tm):
    l, d = x.shape
    return pl.pallas_call(
        functools.partial(_wout_kernel, tm=tm, ctx_len=ctx_len),
        grid=(l // tm,),
        in_specs=[pl.BlockSpec((tm, RET_W), lambda i: (i, 0)),
                  pl.BlockSpec((tm, HY_W), lambda i: (i, 0)),
                  pl.BlockSpec((tm, HG_W), lambda i: (i, 0)),
                  pl.BlockSpec((d, d), lambda i: (0, 0)),
                  pl.BlockSpec((tm, d), lambda i: (i, 0)),
                  pl.BlockSpec((1, d), lambda i: (0, 0)),
                  pl.BlockSpec((8, d), lambda i: (0, 0))],
        out_specs=[pl.BlockSpec((tm, d), lambda i: (i, 0)), pl.BlockSpec((tm, d), lambda i: (i, 0))],
        out_shape=[jax.ShapeDtypeStruct((l, d), F32), jax.ShapeDtypeStruct((l, d), F32)],
        compiler_params=_cparams(("arbitrary",)),
        name="out_proj",
    )(o_ret, o_hy, o_hg, w_bf16, x, gain.reshape(1, d), mod)


def _router_kernel(h_ref, w_ref, b_ref, idx_ref, wt_ref):
    logits = _dot(h_ref[...].astype(BF16), w_ref[...]) + b_ref[...]
    lane = lax.broadcasted_iota(jnp.int32, logits.shape, 1).astype(F32)
    vals, idxs = [], []
    x = logits
    for _ in range(TOP_K):
        mx = jnp.max(x, axis=-1, keepdims=True)
        ix = jnp.min(jnp.where(x == mx, lane, float(LANES)), axis=-1, keepdims=True)
        vals.append(mx)
        idxs.append(ix)
        x = jnp.where(lane == ix, -jnp.inf, x)
    es = [jnp.exp(vv - vals[0]) for vv in vals]
    tot = es[0] + es[1] + es[2] + es[3]
    idx_out = jnp.zeros(logits.shape, F32)
    wt_out = jnp.zeros(logits.shape, F32)
    for r in range(TOP_K):
        idx_out = jnp.where(lane == float(r), idxs[r], idx_out)
        wt_out = jnp.where(lane == float(r), es[r] / tot, wt_out)
    idx_ref[...] = idx_out.astype(jnp.int32)
    wt_ref[...] = wt_out


def router(h2, router_w, router_b, row0, n_tok, tm=256):
    d = h2.shape[1]
    w = jnp.zeros((d, LANES), BF16).at[:, :N_EXPERTS].set(router_w.astype(BF16))
    b = jnp.full((1, LANES), -jnp.inf, F32).at[0, :N_EXPERTS].set(router_b)
    return pl.pallas_call(
        _router_kernel,
        grid=(n_tok // tm,),
        in_specs=[pl.BlockSpec((tm, d), lambda i: (i + row0 // tm, 0)),
                  pl.BlockSpec((d, LANES), lambda i: (0, 0)),
                  pl.BlockSpec((1, LANES), lambda i: (0, 0))],
        out_specs=[pl.BlockSpec((tm, LANES), lambda i: (i, 0)), pl.BlockSpec((tm, LANES), lambda i: (i, 0))],
        out_shape=[jax.ShapeDtypeStruct((n_tok, LANES), jnp.int32), jax.ShapeDtypeStruct((n_tok, LANES), F32)],
        compiler_params=_cparams(("arbitrary",)),
        name="router",
    )(h2, w, b)


def _row_copy(src_hbm, src_row, dst, dst_row, sem):
    return pltpu.make_async_copy(src_hbm.at[pl.ds(src_row, 1)], dst.at[pl.ds(dst_row, 1)], sem)


def _group_wait(src_hbm, dst, dst_row0, sem):
    pltpu.make_async_copy(src_hbm.at[pl.ds(0, GATHER_UNROLL)], dst.at[pl.ds(dst_row0, GATHER_UNROLL)], sem).wait()


def _expert_kernel(blk_e_ref, nused_ref, valid_ref, tok_ref, h_hbm, wg_ref, bg_ref, wu_ref, bu_ref, wd_ref, bd_ref,
                   o_ref, xf_scr, xb_scr, sem, *, row0):
    b = pl.program_id(0)
    f = pl.program_id(1)
    nused = nused_ref[0]

    def gather(blk, wait):
        def body(g, carry):
            r0 = pl.multiple_of(g * GATHER_UNROLL, GATHER_UNROLL)
            if wait:
                _group_wait(h_hbm, xf_scr, r0, sem)
            else:
                for u in range(GATHER_UNROLL):
                    _row_copy(h_hbm, row0 + tok_ref[blk * MOE_TM + r0 + u], xf_scr, r0 + u, sem).start()
            return carry

        n_groups = lax.shift_right_logical(valid_ref[blk] + (GATHER_UNROLL - 1), GATHER_UNROLL.bit_length() - 1)
        lax.fori_loop(0, n_groups, body, 0)

    @pl.when(b < nused)
    def _():
        @pl.when(f == 0)
        def _():
            @pl.when(b == 0)
            def _():
                xf_scr[...] = jnp.zeros_like(xf_scr)
                gather(0, False)

            gather(b, True)
            xb_scr[...] = xf_scr[...].astype(BF16)

        @pl.when((f == 1) & (b + 1 < nused))
        def _():
            gather(b + 1, False)

        x = xb_scr[...]
        gate = jnp.minimum(_dot(x, wg_ref[0].astype(BF16)) + bg_ref[0], SWIGLU_LIMIT)
        up = jnp.clip(_dot(x, wu_ref[0].astype(BF16)) + bu_ref[0], -SWIGLU_LIMIT, SWIGLU_LIMIT)
        act = gate * jax.nn.sigmoid(SWIGLU_ALPHA * gate) * (up + 1.0)
        part = _dot(act.astype(BF16), wd_ref[0].astype(BF16))

        @pl.when(f == 0)
        def _():
            o_ref[...] = part + bd_ref[0]

        @pl.when(f != 0)
        def _():
            o_ref[...] += part

    @pl.when((b >= nused) & (f == 0))
    def _():
        o_ref[...] = jnp.zeros_like(o_ref)


def expert_ffn(h2, row0, slot_tok, blk_e, n_used, blk_valid, layer, w_gate, b_gate, w_up, b_up, w_down, b_down):
    d = h2.shape[1]
    n_rows = slot_tok.shape[0]
    depth, e, _, dff = w_gate.shape
    nb, nf = n_rows // MOE_TM, dff // MOE_TF
    assert nf >= 2 and MOE_TM % GATHER_UNROLL == 0

    def blk(b, nu):
        return jnp.minimum(b, nu[0] - 1)

    def fidx(b, f, nu):
        return jnp.where(b < nu[0], f, nf - 1)

    grid_spec = pltpu.PrefetchScalarGridSpec(
        num_scalar_prefetch=4,
        grid=(nb, nf),
        in_specs=[pl.BlockSpec(memory_space=pl.ANY),
                  pl.BlockSpec((None, 1, d, MOE_TF), lambda b, f, be, nu, vl, tk: (layer, be[blk(b, nu)],0, fidx(b, f, nu))),
                  pl.BlockSpec((None, 1, 1, MOE_TF), lambda b, f, be, nu, vl, tk: (layer, be[blk(b, nu)],0, fidx(b, f, nu))),
                  pl.BlockSpec((None, 1, d, MOE_TF), lambda b, f, be, nu, vl, tk: (layer, be[blk(b, nu)],0, fidx(b, f, nu))),
                  pl.BlockSpec((None, 1, 1, MOE_TF), lambda b, f, be, nu, vl, tk: (layer, be[blk(b, nu)],0, fidx(b, f, nu))),
                  pl.BlockSpec((None, 1, MOE_TF, d), lambda b, f, be, nu, vl, tk: (layer, be[blk(b, nu)],fidx(b, f, nu), 0)),
                  pl.BlockSpec((None, 1, 1, d), lambda b, f, be, nu, vl, tk: (layer, be[blk(b, nu)],0, 0))],
        out_specs=pl.BlockSpec((MOE_TM, d), lambda b, f, be, nu, vl, tk: (b, 0)),
        scratch_shapes=[pltpu.VMEM((MOE_TM, d), F32), pltpu.VMEM((MOE_TM, d), BF16), pltpu.SemaphoreType.DMA],
    )
    return pl.pallas_call(
        functools.partial(_expert_kernel, row0=row0),
        grid_spec=grid_spec,
        out_shape=jax.ShapeDtypeStruct((n_rows, d), F32),
        compiler_params=_cparams(("arbitrary", "arbitrary")),
        name="expert_ffn",
    )(blk_e, n_used, blk_valid, slot_tok, h2, w_gate, b_gate.reshape(depth, e, 1, dff), w_up,
      b_up.reshape(depth, e, 1, dff), w_down, b_down.reshape(depth, e, 1, d))


def _combine_kernel(dest_ref, y_hbm, wt_ref, x_ref, mod_ref, g_ref, o_ref, y_scr, sem, *, tm, row0, ctx_len, final):
    i = pl.program_id(0)

    def gather(tile, wait):
        def body(g, carry):
            r0 = pl.multiple_of(g * GATHER_UNROLL, GATHER_UNROLL)
            for k in range(TOP_K):
                dst = y_scr.at[tile % 2, k]
                if wait:
                    _group_wait(y_hbm, dst, r0, sem.at[tile % 2])
                else:
                    for u in range(GATHER_UNROLL):
                        _row_copy(y_hbm, dest_ref[(tile * tm + r0 + u) * TOP_K + k], dst, r0 + u, sem.at[tile % 2]).start()
            return carry

        lax.fori_loop(0, tm // GATHER_UNROLL, body, 0)

    @pl.when(i == 0)
    def _():
        gather(0, False)

    @pl.when(i + 1 < pl.num_programs(0))
    def _():
        gather(i + 1, False)

    gather(i, True)
    wt = wt_ref[...]
    ys = y_scr.at[i % 2]
    acc = ys[0] * wt[:, 0:1]
    for k in range(1, TOP_K):
        acc = acc + ys[k] * wt[:, k:k + 1]
    row = row0 + i * tm + lax.broadcasted_iota(jnp.int32, (tm, 1), 0)
    x = x_ref[...] + jnp.where(row < ctx_len, mod_ref[0:1, :], mod_ref[1:2, :]) * acc
    if final:
        x = x * lax.rsqrt(jnp.mean(x * x, axis=-1, keepdims=True) + EPS) * g_ref[...]
    o_ref[...] = x


def moe_combine(ys, dest, top_w, x, mod, final_gain, row0, n_tok, ctx_len, final, tm=128):
    d = x.shape[1]
    grid_spec = pltpu.PrefetchScalarGridSpec(
        num_scalar_prefetch=1,
        grid=(n_tok // tm,),
        in_specs=[pl.BlockSpec(memory_space=pl.ANY),
                  pl.BlockSpec((tm, LANES), lambda i, ds: (i, 0)),
                  pl.BlockSpec((tm, d), lambda i, ds: (i + row0 // tm, 0)),
                  pl.BlockSpec((8, d), lambda i, ds: (0, 0)),
                  pl.BlockSpec((1, d), lambda i, ds: (0, 0))],
        out_specs=pl.BlockSpec((tm, d), lambda i, ds: (i, 0)),
        scratch_shapes=[pltpu.VMEM((2, TOP_K, tm, d), F32), pltpu.SemaphoreType.DMA((2,))],
    )
    return pl.pallas_call(
        functools.partial(_combine_kernel, tm=tm, row0=row0, ctx_len=ctx_len, final=final),
        grid_spec=grid_spec,
        out_shape=jax.ShapeDtypeStruct((n_tok, d), F32),
        compiler_params=_cparams(("arbitrary",)),
        name="moe_combine",
    )(dest, ys, top_w, x, mod, final_gain.reshape(1, d))


FFT_R = 128
FFT_N = FFT_R * FFT_R
HY_TC = 128
HY_TCB = HY_TC


def _hypre_kernel(z_ref, zp_ref, zn_ref, w_ref, b_ref, oc_ref, ol_ref, *, tm, ctx_len, l_tot):
    i = pl.program_id(1)
    x = z_ref[...]
    loc = lax.broadcasted_iota(jnp.int32, (tm, 1), 0)
    row = i * tm + loc
    prev = jnp.where(loc == 0, zp_ref[7:8, :], pltpu.roll(x, 1, 0))
    prev = jnp.where((row == 0) | (row == ctx_len), 0.0, prev)
    nxt = jnp.where(loc == tm - 1, zn_ref[0:1, :], pltpu.roll(x, tm - 1, 0))
    nxt = jnp.where((row == ctx_len - 1) | (row == l_tot - 1), 0.0, nxt)
    y = w_ref[0:1, :] * prev + w_ref[1:2, :] * x + w_ref[2:3, :] * nxt + b_ref[...]

    @pl.when(i == 0)
    def _():
        oc_ref[...] = y

    @pl.when(i > 0)
    def _():
        ol_ref[...] = y


def hyena_pre(z, conv_w, conv_b, ctx_len):
    l = z.shape[0]
    tm, tc = ctx_len, HY_COLS
    assert RET_COLS % tc == 0
    cb0 = RET_COLS // tc
    nr8 = l // 8
    return pl.pallas_call(
        functools.partial(_hypre_kernel, tm=tm, ctx_len=ctx_len, l_tot=l),
        grid=(HY_COLS // tc, l // tm),
        in_specs=[pl.BlockSpec((tm, tc), lambda c, i: (i, cb0 + c)),
                  pl.BlockSpec((8, tc), lambda c, i: (jnp.maximum(i * (tm // 8) - 1, 0), cb0 + c)),
                  pl.BlockSpec((8, tc), lambda c, i: (jnp.minimum((i + 1) * (tm // 8), nr8 - 1), cb0 + c)),
                  pl.BlockSpec((3, tc), lambda c, i: (0, c)),
                  pl.BlockSpec((1, tc), lambda c, i: (0, c))],
        out_specs=[pl.BlockSpec((tm, tc), lambda c, i: (0, c)),
                   pl.BlockSpec((tm, tc), lambda c, i: (jnp.maximum(i - 1, 0), c))],
        out_shape=[jax.ShapeDtypeStruct((ctx_len, HY_COLS), F32), jax.ShapeDtypeStruct((l - ctx_len, HY_COLS), F32)],
        compiler_params=_cparams(("arbitrary", "arbitrary")),
        name="hyena_pre",
    )(z, z, z, conv_w, conv_b.reshape(1, HY_COLS))


def _hyfilt_kernel(ft_ref, aux_ref, w1_ref, b1_ref, fr_ref, w2_ref, b2_ref, w3_ref, dl_ref, o_ref):
    fr = fr_ref[...]
    h = jnp.sin(fr * (_dot(ft_ref[...].astype(BF16), w1_ref[...].astype(BF16)) + b1_ref[...]))
    h = jnp.sin(fr * (_dot(h.astype(BF16), w2_ref[...].astype(BF16)) + b2_ref[...]))
    h = _dot(h.astype(BF16), w3_ref[...].astype(BF16))
    aux = aux_ref[...]
    t, m_f, m_b = aux[:, 0:1], aux[:, 1:2], aux[:, 2:3]
    window = jnp.exp(-t * dl_ref[...])
    for o in range(2):
        base = o * 2 * HY_W
        o_ref[:, o * HY_W:(o + 1) * HY_W] = (m_f * h[:, base:base + HY_W] + m_b * h[:, base + HY_W:base + 2 * HY_W]) * window


def hyena_filters(seq_len, circle, w1, b1, freq, w2, b2, w3):
    L = seq_len
    if circle:
        r = np.arange(FFT_N)
        is_f, is_b = r < L, r > FFT_N - L
        pos = np.where(is_f, r, np.where(is_b, FFT_N - r, 0))
    else:
        lag = np.arange(2 * L) - L
        is_f, is_b = lag >= 0, (lag < 0) & (lag > -L)
        pos = np.where(is_b, -lag, np.where(is_f, lag, 0))
    n_rows = pos.shape[0]
    t_lin = np.linspace(0.0, 1.0, L, dtype=np.float32)[:, None]
    w_ang = np.float32(2.0 * math.pi) * np.arange(L, dtype=np.float32)[:, None] / np.float32(L)
    f = np.linspace(1e-4, HY_BANDS - 1, HY_BANDS, dtype=np.float32)[None, :]
    feats = np.concatenate([t_lin, np.cos(f * w_ang), -np.sin(f * w_ang)], axis=-1)[pos]
    ne, nw = feats.shape[1], w1.shape[1]
    feats = np.pad(feats, ((0, 0), (0, LANES - ne)))
    aux = np.zeros((n_rows, LANES), np.float32)
    aux[:, 0], aux[:, 1], aux[:, 2] = t_lin[pos, 0], is_f, is_b
    w1p = jnp.pad(w1, ((0, LANES - ne), (0, LANES - nw)))
    w2p = jnp.pad(w2, ((0, LANES - nw), (0, LANES - nw)))
    w3p = jnp.pad(w3, ((0, LANES - nw), (0, 0)))
    pad1 = lambda a: jnp.pad(a, (0, LANES - nw)).reshape(1, LANES)
    deltas = np.abs(np.linspace(HY_MIN_DECAY, HY_MAX_DECAY, HY_W, dtype=np.float32)).reshape(1, HY_W)
    tm = min(1024, n_rows)
    full = lambda shape: pl.BlockSpec(shape, lambda i: (0,) * len(shape))
    return pl.pallas_call(
        _hyfilt_kernel,
        grid=(n_rows // tm,),
        in_specs=[pl.BlockSpec((tm, LANES), lambda i: (i, 0)), pl.BlockSpec((tm, LANES), lambda i: (i, 0)),
                  full((LANES, LANES)), full((1, LANES)), full((1, LANES)), full((LANES, LANES)), full((1, LANES)),
                  full((LANES, 4 * HY_W)), full((1, HY_W))],
        out_specs=pl.BlockSpec((tm, 2 * HY_W), lambda i: (i, 0)),
        out_shape=jax.ShapeDtypeStruct((n_rows, 2 * HY_W), F32),
        compiler_params=_cparams(("arbitrary",)),
        name="hyena_filters",
    )(jnp.asarray(feats), jnp.asarray(aux), w1p, pad1(b1), pad1(freq), w2p, pad1(b2), w3p, jnp.asarray(deltas))


def _dft_consts():
    k = np.arange(FFT_R)
    ang = 2.0 * np.pi * np.outer(k, k) / FFT_R
    c, s = np.cos(ang), np.sin(ang)
    fwd_a = np.concatenate([c, -s], axis=0)
    fwd_b = np.block([[c, s], [-s, c]])
    inv_b = np.block([[c, -s], [s, c]])
    inv_a = np.concatenate([c, -s], axis=1) / FFT_N
    tw = 2.0 * np.pi * np.outer(k, k) / FFT_N
    as32 = lambda a: jnp.asarray(a, F32)
    return (as32(fwd_a), as32(fwd_b), as32(inv_b), as32(inv_a),
            as32(np.cos(tw))[:, :, None], as32(-np.sin(tw))[:, :, None])


def _store_grouped(o_ref, idx, res):
    for part in range(2):
        for g in range(FFT_R // 8):
            o_ref[g, idx, part] = res[part * FFT_R + g * 8:part * FFT_R + (g + 1) * 8, :]


def _fft_a_kernel(x_ref, f_ref, o_ref, *, n1, natural):
    f = f_ref[...].astype(BF16)
    for n2 in range(FFT_R):
        xs = x_ref[pl.ds(n2, n1, stride=FFT_R), :] if natural else x_ref[n2]
        _store_grouped(o_ref, n2, _dot(f, xs.astype(BF16)))


def fft_stage_a(x, n1, natural, c, col0=0):
    fwd_a = _dft_consts()[0][:, :n1]
    xspec = (pl.BlockSpec((n1 * FFT_R, HY_TC), lambda j: (0, col0 + j)) if natural
             else pl.BlockSpec((FFT_R, n1, HY_TC), lambda j: (0, 0, col0 + j)))
    out = pl.pallas_call(
        functools.partial(_fft_a_kernel, n1=n1, natural=natural),
        grid=(c // HY_TC,),
        in_specs=[xspec, pl.BlockSpec((2 * FFT_R, n1), lambda j: (0, 0))],
        out_specs=pl.BlockSpec((FFT_R // 8, FFT_R, 2, 8, HY_TC), lambda j: (0, 0, 0, 0, j)),
        out_shape=jax.ShapeDtypeStruct((FFT_R // 8, FFT_R, 2, 8, c), F32),
        compiler_params=_cparams(("arbitrary",)),
        name="fft_stage_a",
    )(x, fwd_a)
    return out.reshape(FFT_R // 8, FFT_R * 16, c)


def _load_pair(ref, j):
    return jnp.concatenate([ref[pl.ds(j, FFT_R, stride=16), :], ref[pl.ds(8 + j, FFT_R, stride=16), :]], axis=0)


def _twiddle(v, tr, ti):
    vr, vi = v[:FFT_R], v[FFT_R:]
    return jnp.concatenate([vr * tr - vi * ti, vr * ti + vi * tr], axis=0)


def _fft_b_spec_kernel(a_ref, twr_ref, twi_ref, fb_ref, o_ref):
    fb = fb_ref[...].astype(BF16)
    for kk in range(8):
        p = _twiddle(_load_pair(a_ref, kk), twr_ref[kk], twi_ref[kk])
        o_ref[kk] = _dot(fb, p.astype(BF16))


def _fft_b_conv_kernel(a_ref, h_ref, twr_ref, twi_ref, fb_ref, fib_ref, o_ref):
    fb = fb_ref[...].astype(BF16)
    fib = fib_ref[...].astype(BF16)
    for kk in range(8):
        tr, ti = twr_ref[kk], twi_ref[kk]
        x = _dot(fb, _twiddle(_load_pair(a_ref, kk), tr, ti).astype(BF16))
        h = h_ref[kk]
        y = _twiddle(x, h[:FFT_R], h[FFT_R:])
        z = _twiddle(_dot(fib, y.astype(BF16)), tr, -ti)
        for part in range(2):
            for g in range(FFT_R // 8):
                o_ref[g, kk, part] = z[part * FFT_R + g * 8:part * FFT_R + (g + 1) * 8, :]


def _bspecs(c):
    del c
    return [pl.BlockSpec((None, FFT_R * 16, HY_TCB), lambda j, g: (g, 0, j))]


def fft_stage_b_spec(a):
    c = a.shape[-1]
    _, fwd_b, _, _, twr, twi = _dft_consts()
    tws = pl.BlockSpec((8, FFT_R, 1), lambda j, g: (g, 0, 0))
    return pl.pallas_call(
        _fft_b_spec_kernel,
        grid=(c // HY_TCB, FFT_R // 8),
        in_specs=_bspecs(c) + [tws, tws, pl.BlockSpec((2 * FFT_R, 2 * FFT_R), lambda j, g: (0, 0))],
        out_specs=pl.BlockSpec((8, 2 * FFT_R, HY_TCB), lambda j, g: (g, 0, j)),
        out_shape=jax.ShapeDtypeStruct((FFT_R, 2 * FFT_R, c), F32),
        compiler_params=_cparams(("arbitrary", "arbitrary")),
        name="fft_stage_b_spec",
    )(a, twr, twi, fwd_b)


def fft_stage_b_conv(a, hspec, h_col0):
    c = a.shape[-1]
    _, fwd_b, inv_b, _, twr, twi = _dft_consts()
    tws = pl.BlockSpec((8, FFT_R, 1), lambda j, g: (g, 0, 0))
    sq = pl.BlockSpec((2 * FFT_R, 2 * FFT_R), lambda j, g: (0, 0))
    out = pl.pallas_call(
        _fft_b_conv_kernel,
        grid=(c // HY_TCB, FFT_R // 8),
        in_specs=_bspecs(c) + [pl.BlockSpec((8, 2 * FFT_R, HY_TCB), lambda j, g: (g, 0, h_col0 + j)), tws, tws, sq, sq],
        out_specs=pl.BlockSpec((FFT_R // 8, 8, 2, 8, HY_TCB), lambda j, g: (0, g, 0, 0, j)),
        out_shape=jax.ShapeDtypeStruct((FFT_R // 8, FFT_R, 2, 8, c), F32),
        compiler_params=_cparams(("arbitrary", "arbitrary")),
        name="fft_stage_b_conv",
    )(a, hspec, twr, twi, fwd_b, inv_b)
    return out.reshape(FFT_R // 8, FFT_R * 16, c)


def _fft_ainv_kernel(z_ref, u_ref, gate_ref, skip_ref, fi_ref, o_ref, *, n1, u_natural):
    g2 = pl.program_id(1)
    fi = fi_ref[...].astype(BF16)
    span = (n1 - 1) * FFT_R + 8
    base = pl.multiple_of(g2 * 8, 8)
    gate_rows = gate_ref.at[pl.ds(base, span)]
    u_rows = u_ref.at[pl.ds(base, span)] if u_natural else None
    for jj in range(8):
        y = _dot(fi, _load_pair(z_ref, jj).astype(BF16))
        u = u_rows[pl.ds(jj, n1, stride=FFT_R), :] if u_natural else u_ref[jj]
        o_ref[jj] = gate_rows[pl.ds(jj, n1, stride=FFT_R), :] * (y + skip_ref[...] * u)


def fft_stage_a_inv(z, u, gate, skip, gate_col, n1, u_natural):
    c = z.shape[-1]
    inv_a = _dft_consts()[3][:n1]
    ctiles = c // HY_TCB
    uspec = (pl.BlockSpec((n1 * FFT_R, HY_TCB), lambda j, g: (0, j)) if u_natural
             else pl.BlockSpec((8, n1, HY_TCB), lambda j, g: (g, 0, j)))
    return pl.pallas_call(
        functools.partial(_fft_ainv_kernel, n1=n1, u_natural=u_natural),
        grid=(ctiles, FFT_R // 8),
        in_specs=[pl.BlockSpec((None, FFT_R * 16, HY_TCB), lambda j, g: (g, 0, j)),
                  uspec,
                  pl.BlockSpec((n1 * FFT_R, HY_TCB), lambda j, g: (0, gate_col * ctiles + j)),
                  pl.BlockSpec((1, HY_TCB), lambda j, g: (0, j)),
                  pl.BlockSpec((n1, 2 * FFT_R), lambda j, g: (0, 0))],
        out_specs=pl.BlockSpec((8, n1, HY_TCB), lambda j, g: (g, 0, j)),
        out_shape=jax.ShapeDtypeStruct((FFT_R, n1, c), F32),
        compiler_params=_cparams(("arbitrary", "arbitrary")),
        name="fft_stage_a_inv",
    )(z, u, gate, skip.reshape(1, c), inv_a)


def hyena_long(u3, filt, skip):
    n1 = u3.shape[0] // FFT_R
    ct = HY_W // HY_TCB
    hspec = fft_stage_b_spec(fft_stage_a(filt, FFT_R, True, 2 * HY_W))
    a = fft_stage_a(u3, n1, True, HY_W)
    y1 = fft_stage_a_inv(fft_stage_b_conv(a, hspec, 0), u3, u3, skip[0], 1, n1, True)
    a = fft_stage_a(y1, n1, False, HY_W)
    return fft_stage_a_inv(fft_stage_b_conv(a, hspec, ct), y1, u3, skip[1], 2, n1, False)


def _hyctx_kernel(u_ref, h_ref, skip_ref, o_ref, vs_scr, *, seq_len):
    L = seq_len
    zeros = jnp.zeros((L, HY_TC), F32)

    def conv(u, order):
        padded = jnp.concatenate([zeros, u, zeros], axis=0)
        for b in range(8):
            vs_scr[b] = padded if b == 0 else pltpu.roll(padded, b, 0)

        def body(a, acc):
            taps = h_ref[pl.ds(pl.multiple_of(8 * a, 8), 8), order * HY_TC:(order + 1) * HY_TC]
            start = pl.multiple_of(2 * L - 8 * a, 8)
            for b in range(8):
                acc = acc + taps[b:b + 1, :] * vs_scr[b, pl.ds(start, L), :]
            return acc

        return lax.fori_loop(0, 2 * L // 8, body, zeros)

    v, x1, x2 = u_ref[:, 0:HY_TC], u_ref[:, HY_TC:2 * HY_TC], u_ref[:, 2 * HY_TC:3 * HY_TC]
    y = x1 * (conv(v, 0) + skip_ref[0:1, :] * v)
    o_ref[...] = x2 * (conv(y, 1) + skip_ref[1:2, :] * y)


def hyena_short(u3, lag_filt, skip):
    L = u3.shape[0]
    ct = HY_W // HY_TC
    u_t = u3.reshape(L, 3, ct, HY_TC).transpose(0, 2, 1, 3).reshape(L, ct * 3 * HY_TC)
    h_t = lag_filt.reshape(2 * L, 2, ct, HY_TC).transpose(0, 2, 1, 3).reshape(2 * L, ct * 2 * HY_TC)
    return pl.pallas_call(
        functools.partial(_hyctx_kernel, seq_len=L),
        grid=(ct,),
        in_specs=[pl.BlockSpec((L, 3 * HY_TC), lambda j: (0, j)),
                  pl.BlockSpec((2 * L, 2 * HY_TC), lambda j: (0, j)),
                  pl.BlockSpec((2, HY_TC), lambda j: (0, j))],
        out_specs=pl.BlockSpec((L, HY_TC), lambda j: (0, j)),
        out_shape=jax.ShapeDtypeStruct((L, HY_W), F32),
        scratch_shapes=[pltpu.VMEM((8, 3 * L, HY_TC), F32)],
        compiler_params=_cparams(("arbitrary",)),
        name="hyena_short",
    )(u_t, h_t, skip)


def _unpermute(y):
    return jnp.swapaxes(y, 0, 1).reshape(-1, y.shape[-1])


def _rope_tables(ctx_len, n_rows):
    nf = HEAD_DIM // 4
    inv = ROPE_BASE ** (-jnp.arange(nf, dtype=F32) / nf)
    rows = jnp.repeat(jnp.arange(n_rows, dtype=F32), GRID_W)
    cols = (jnp.arange(n_rows * GRID_W) % GRID_W).astype(F32)
    ang = jnp.concatenate([rows[:, None] * inv, cols[:, None] * inv], axis=-1)
    ang = jnp.concatenate([jnp.zeros((ctx_len, HEAD_DIM // 2), F32), ang], axis=0)
    cos_t, sin_t = jnp.cos(ang), jnp.sin(ang)
    return jnp.concatenate([cos_t, cos_t], axis=-1), jnp.concatenate([-sin_t, sin_t], axis=-1)


def _moe(h2, x, mod_gate, final_gain, layer, moe_p, row0, n_tok, ctx_len, final):
    router_w, router_b, w_gate, b_gate, w_up, b_up, w_down, b_down = moe_p
    router_w, router_b = router_w[layer], router_b[layer]
    top_idx_p, top_w_p = router(h2, router_w, router_b, row0, n_tok)
    top_idx = top_idx_p[:, :TOP_K]
    onehot = (top_idx[:, :, None] == jnp.arange(N_EXPERTS)[None, None, :]).astype(jnp.int32)
    per_tok = onehot.sum(axis=1)
    before = jnp.cumsum(per_tok, axis=0) - per_tok
    counts = per_tok.sum(axis=0)
    nblk = (counts + MOE_TM - 1) // MOE_TM
    blk_end = jnp.cumsum(nblk)
    pad_start = (blk_end - nblk) * MOE_TM
    dest = (pad_start[top_idx] + jnp.take_along_axis(before, top_idx, axis=1)).astype(jnp.int32)
    n_blocks = -(-(n_tok * TOP_K + N_EXPERTS * (MOE_TM - 1)) // MOE_TM)
    blk_e = jnp.minimum(jnp.searchsorted(blk_end, jnp.arange(n_blocks), side='right'), N_EXPERTS - 1)
    n_used = blk_end[-1:].astype(jnp.int32)
    blk_first = (blk_end - nblk)[blk_e]
    blk_valid = jnp.clip(counts[blk_e] - (jnp.arange(n_blocks) - blk_first) * MOE_TM, 0, MOE_TM).astype(jnp.int32)
    tok = jnp.broadcast_to(jnp.arange(n_tok, dtype=jnp.int32)[:, None], dest.shape)
    slot_tok = jnp.zeros((n_blocks * MOE_TM,), jnp.int32).at[dest.reshape(-1)].set(tok.reshape(-1))
    ys = expert_ffn(h2, row0, slot_tok, blk_e.astype(jnp.int32), n_used, blk_valid, layer, w_gate, b_gate, w_up, b_up,
                    w_down, b_down)
    return moe_combine(ys, dest.reshape(-1), top_w_p, x, mod_gate, final_gain, row0, n_tok, ctx_len, final)


def kernel(x, c, ctx, c_ctx, norm_mix_g, norm_ffn_g, final_norm_g, ada_w, ada_b, w_in, w_out, ret_decay_logit,
           hy_conv_w, hy_conv_b, hy_filt_w1, hy_filt_b1, hy_filt_freq, hy_filt_w2, hy_filt_b2, hy_filt_w3, hy_skip,
           hg_lb_logits, router_w, router_b, moe_w_gate, moe_b_gate, moe_w_up, moe_b_up, moe_w_down, moe_b_down):
    bsz, seq, d = x.shape
    assert bsz == 1
    ctx_len = ctx.shape[1]
    l = ctx_len + seq
    cos2, sin2 = _rope_tables(ctx_len, seq // GRID_W)
    cvec = jnp.zeros((8, d), F32).at[0].set(c_ctx).at[1].set(c[0])
    mod = adaln(cvec, ada_w, ada_b)
    xs = jnp.concatenate([ctx[0], x[0]], axis=0)
    p_lb = jax.nn.softmax(hg_lb_logits.astype(F32), axis=0)
    lb_cum = jnp.cumsum(p_lb, axis=0)
    for layer in range(DEPTH):
        last = layer == DEPTH - 1
        m = mod[layer].reshape(8, N_MOD, d)
        mc, ml = m[0], m[1]
        mod_in = jnp.stack([mc[0], mc[1], ml[0], ml[1], mc[0], mc[0], mc[0], mc[0]])
        z = in_proj(xs, norm_mix_g[layer], mod_in, w_in[layer].astype(BF16), ctx_len, tm=768, tn=1408)
        log_gamma = jax.nn.log_sigmoid(ret_decay_logit[layer].astype(F32))
        o_ret = retention_dir(z, log_gamma, cos2, sin2, None, ctx_len, reverse=False)
        o_ret = retention_dir(z, log_gamma, cos2, sin2, o_ret, ctx_len, reverse=True)
        lb = lb_cum[layer] - p_lb[0]
        o_hg = hgrn2_dir(z, lb, None, ctx_len, reverse=False)
        o_hg = hgrn2_dir(z, lb, o_hg, ctx_len, reverse=True)
        filt_p = (hy_filt_w1[layer], hy_filt_b1[layer], hy_filt_freq[layer], hy_filt_w2[layer], hy_filt_b2[layer],
                  hy_filt_w3[layer])
        u_ctx, u_lat = hyena_pre(z, hy_conv_w[layer], hy_conv_b[layer], ctx_len)
        o_hy_l = _unpermute(hyena_long(u_lat, hyena_filters(seq, True, *filt_p), hy_skip[layer]))
        if last:
            o_hy_c = jnp.zeros((ctx_len, HY_W), F32)
        else:
            o_hy_c = hyena_short(u_ctx, hyena_filters(ctx_len, False, *filt_p), hy_skip[layer])
        o_hy = jnp.concatenate([o_hy_c, o_hy_l], axis=0)
        mod_out = jnp.stack([mc[2], ml[2], mc[3], mc[4], ml[3], ml[4], mc[0], mc[0]])
        xs, h2 = out_proj(o_ret, o_hy, o_hg, w_out[layer].astype(BF16), xs, norm_ffn_g[layer], mod_out, ctx_len,
                          tm=384)
        moe_p = (router_w, router_b, moe_w_gate, moe_b_gate, moe_w_up, moe_b_up, moe_w_down, moe_b_down)
        mod_gate = jnp.stack([mc[5], ml[5], mc[0], mc[0], mc[0], mc[0], mc[0], mc[0]])
        if last:
            return _moe(h2, xs, mod_gate, final_norm_g, layer, moe_p, ctx_len, seq, ctx_len, True)[None]
        xs = _moe(h2, xs, mod_gate, final_norm_g, layer, moe_p, 0, l, ctx_len, False)
```

```python
import functools
import math

import jax
import jax.numpy as jnp
import numpy as np
from jax import lax
from jax.experimental import pallas as pl
from jax.experimental.pallas import tpu as pltpu

F32 = jnp.float32
BF16 = jnp.bfloat16

D_MODEL = 2048
DEPTH = 2
GRID_W = 64
HEAD_DIM = 128
RET_W = 3 * D_MODEL // 8
RET_H = RET_W // HEAD_DIM
HY_W = D_MODEL // 4
HG_W = D_MODEL - RET_W - HY_W
HG_H = HG_W // HEAD_DIM
RET_COLS = 4 * RET_W
HY_COLS = 3 * HY_W
HG_COLS = 5 * HG_W
IN_W = RET_COLS + HY_COLS + HG_COLS
ROPE_BASE = 10000.0
EPS = 1e-6
N_MOD = 6
HY_BANDS = 16
HY_MAX_DECAY = math.log(1e-2) / 0.3
HY_MIN_DECAY = math.log(1e-2) / 1.5
N_EXPERTS = 32
TOP_K = 4
SWIGLU_LIMIT = 7.0
SWIGLU_ALPHA = 1.702

LANES = 128
VMEM_LIMIT = 56 * 1024 * 1024

RET_CHUNK = 256
RET_HEADS_PER_STEP = 6
HG_CHUNK = 128
HG_SUB = 16
HG_HEADS_PER_STEP = 6
MOE_TM = 576
GATHER_UNROLL = 8
MOE_TF = 512


def _cparams(sem):
    return pltpu.CompilerParams(dimension_semantics=sem, vmem_limit_bytes=VMEM_LIMIT)


def _dot(a, b):
    return jnp.dot(a, b, preferred_element_type=F32)


def _dot_nt(a, b):
    return lax.dot_general(a, b, (((1,), (1,)), ((), ())), preferred_element_type=F32)


def _adaln_kernel(c_ref, w_ref, b_ref, o_ref):
    c = c_ref[...]
    cond = c * jax.nn.sigmoid(c)
    o_ref[0] = _dot(cond.astype(BF16), w_ref[0].astype(BF16)) + b_ref[0]


def adaln(cvec, ada_w, ada_b, tn=1536):
    depth, d, n = ada_w.shape
    return pl.pallas_call(
        _adaln_kernel,
        grid=(depth, n // tn),
        in_specs=[pl.BlockSpec((8, d), lambda l, j: (0, 0)),
                  pl.BlockSpec((1, d, tn), lambda l, j: (l, 0, j)),
                  pl.BlockSpec((1, 1, tn), lambda l, j: (l, 0, j))],
        out_specs=pl.BlockSpec((1, 8, tn), lambda l, j: (l, 0, j)),
        out_shape=jax.ShapeDtypeStruct((depth, 8, n), F32),
        compiler_params=_cparams(("arbitrary", "arbitrary")),
        name="adaln",
    )(cvec, ada_w, ada_b.reshape(depth, 1, n))


def _norm_mod(x, gain, mod_ref, row0, ctx_len, shift_row):
    y = x * lax.rsqrt(jnp.mean(x * x, axis=-1, keepdims=True) + EPS) * gain
    row = row0 + lax.broadcasted_iota(jnp.int32, (x.shape[0], 1), 0)
    is_ctx = row < ctx_len
    shift = jnp.where(is_ctx, mod_ref[shift_row:shift_row + 1, :], mod_ref[shift_row + 2:shift_row + 3, :])
    scale = jnp.where(is_ctx, mod_ref[shift_row + 1:shift_row + 2, :], mod_ref[shift_row + 3:shift_row + 4, :])
    return y * (1.0 + scale) + shift


def _win_kernel(x_ref, g_ref, mod_ref, w_ref, o_ref, h_scr, *, tm, ctx_len):
    i = pl.program_id(0)

    @pl.when(pl.program_id(1) == 0)
    def _():
        h_scr[...] = _norm_mod(x_ref[...], g_ref[...], mod_ref, i * tm, ctx_len, 0).astype(BF16)

    o_ref[...] = _dot(h_scr[...], w_ref[...])


def in_proj(x, gain, mod, w_bf16, ctx_len, tm, tn):
    l, d = x.shape
    n = w_bf16.shape[1]
    return pl.pallas_call(
        functools.partial(_win_kernel, tm=tm, ctx_len=ctx_len),
        grid=(l // tm, n // tn),
        in_specs=[pl.BlockSpec((tm, d), lambda i, j: (i, 0)),
                  pl.BlockSpec((1, d), lambda i, j: (0, 0)),
                  pl.BlockSpec((8, d), lambda i, j: (0, 0)),
                  pl.BlockSpec((d, tn), lambda i, j: (0, j))],
        out_specs=pl.BlockSpec((tm, tn), lambda i, j: (i, j)),
        out_shape=jax.ShapeDtypeStruct((l, n), F32),
        scratch_shapes=[pltpu.VMEM((tm, d), BF16)],
        compiler_params=_cparams(("arbitrary", "arbitrary")),
        name="in_proj",
    )(x, gain.reshape(1, d), mod, w_bf16)


def _chunk_of_step(i, n_ctx, n_tot, reverse):
    if not reverse:
        return i
    return jnp.where(i < n_ctx, n_ctx - 1 - i, n_tot - 1 - (i - n_ctx))


def _head_norm_gate(o, gate):
    return gate * (o * lax.rsqrt(jnp.mean(o * o, axis=-1, keepdims=True) + EPS))


def _ret_kernel(lg_ref, q_ref, k_ref, v_ref, cos_ref, sin_ref, *rest, reverse, final, chunk, heads):
    if final:
        of_ref, g_ref, o_ref, st_scr = rest
    else:
        o_ref, st_scr = rest
    c = chunk

    @pl.when(pl.program_id(1) == 0)
    def _():
        st_scr[...] = jnp.zeros_like(st_scr)

    cos2 = cos_ref[...]
    sin2 = sin_ref[...]

    def rope(x):
        return x * cos2 + pltpu.roll(x, HEAD_DIM // 2, 1) * sin2

    t = lax.broadcasted_iota(jnp.int32, (c, c), 0)
    s = lax.broadcasted_iota(jnp.int32, (c, c), 1)
    dist = (s - t) if reverse else (t - s)
    a = lax.broadcasted_iota(jnp.int32, (c, 1), 0)
    steps_q = ((c - a) if reverse else (a + 1)).astype(F32)
    steps_k = (a if reverse else (c - 1 - a)).astype(F32)
    for hh in range(heads):
        cols = slice(hh * HEAD_DIM, (hh + 1) * HEAD_DIM)
        lg = lg_ref[1 if reverse else 0, pl.program_id(0) * heads + hh]
        q = rope(q_ref[:, cols]).astype(BF16)
        k = rope(k_ref[:, cols]) * (HEAD_DIM ** -0.5)
        v = v_ref[:, cols]
        dec = jnp.where(dist >= 0, jnp.exp(lg * jnp.maximum(dist, 0).astype(F32)), 0.0)
        p = (_dot_nt(q, k.astype(BF16)) * dec).astype(BF16)
        st = st_scr[hh]
        o = _dot(p, v.astype(BF16)) + jnp.exp(lg * steps_q) * _dot_nt(q, st.astype(BF16))
        kd = (k * jnp.exp(lg * steps_k)).astype(BF16)
        st_scr[hh] = jnp.exp(lg * c) * st + _dot(v.T.astype(BF16), kd)
        if final:
            g = g_ref[:, cols]
            o_ref[:, cols] = _head_norm_gate(o + of_ref[:, cols], g * jax.nn.sigmoid(g)).astype(o_ref.dtype)
        else:
            o_ref[:, cols] = o


def retention_dir(z, log_gamma, cos2, sin2, o_fwd, ctx_len, reverse):
    l = z.shape[0]
    c, heads = RET_CHUNK, RET_HEADS_PER_STEP
    w = heads * HEAD_DIM
    n_tot, n_ctx = l // c, ctx_len // c
    hb = RET_W // w

    def zspec(col):
        return pl.BlockSpec((c, w), lambda h, i: (_chunk_of_step(i, n_ctx, n_tot, reverse), col + h))

    tspec = pl.BlockSpec((c, HEAD_DIM), lambda h, i: (_chunk_of_step(i, n_ctx, n_tot, reverse), 0))
    ospec = pl.BlockSpec((c, w), lambda h, i: (_chunk_of_step(i, n_ctx, n_tot, reverse), h))
    final = o_fwd is not None
    in_specs = [pl.BlockSpec(memory_space=pltpu.SMEM), zspec(0), zspec(hb), zspec(2 * hb), tspec, tspec]
    args = [log_gamma, z, z, z, cos2, sin2]
    if final:
        in_specs += [ospec, zspec(3 * hb)]
        args += [o_fwd, z]
    return pl.pallas_call(
        functools.partial(_ret_kernel, reverse=reverse, final=final, chunk=c, heads=heads),
        grid=(hb, n_tot),
        in_specs=in_specs,
        out_specs=ospec,
        out_shape=jax.ShapeDtypeStruct((l, RET_W), BF16 if final else F32),
        scratch_shapes=[pltpu.VMEM((heads, HEAD_DIM, HEAD_DIM), F32)],
        compiler_params=_cparams(("arbitrary", "arbitrary")),
        name="retention_bwd" if reverse else "retention_fwd",
    )(*args)


def _hg_kernel(q_ref, fz_ref, v_ref, lb_ref, tri_ref, sel_ref, *rest, reverse, final, chunk, sub, heads):
    if final:
        of_ref, g_ref, o_ref, st_scr, cum_scr, k_scr, p_scr = rest
    else:
        o_ref, st_scr, cum_scr, k_scr, p_scr = rest
    c = chunk

    @pl.when(pl.program_id(1) == 0)
    def _():
        st_scr[...] = jnp.zeros_like(st_scr)

    row = lax.broadcasted_iota(jnp.int32, (c, 1), 0)
    t = lax.broadcasted_iota(jnp.int32, (c, c), 0)
    s = lax.broadcasted_iota(jnp.int32, (c, c), 1)

    for hh in range(heads):
        cols = slice(hh * HEAD_DIM, (hh + 1) * HEAD_DIM)
        cum_h, k_h, p_h = cum_scr.at[hh], k_scr.at[hh], p_scr.at[hh]
        qr = q_ref[:, cols]
        q = qr * jax.nn.sigmoid(qr)
        lb = lb_ref[0][:, cols]
        sg = jax.nn.sigmoid(fz_ref[:, cols])
        logf = jnp.log(lb + (1.0 - lb) * sg)
        kk = (1.0 - lb) * (1.0 - sg)
        v32 = v_ref[:, cols]
        v = v32.astype(BF16)
        tri = tri_ref[...]
        f_hi = logf.astype(BF16)
        rest = logf - f_hi.astype(F32)
        f_mid = rest.astype(BF16)
        f_lo = (rest - f_mid.astype(F32)).astype(BF16)
        cum = _dot(tri, f_hi) + _dot(tri, f_mid) + _dot(tri, f_lo)
        cum_h[...] = cum
        k_h[...] = kk

        a_mat = jnp.zeros((c, c), F32)
        m = c // 2
        while m >= sub:
            parts = []
            for p in range(c // (2 * m)):
                r = p * 2 * m + (m if reverse else m - 1)
                parts.append(jnp.broadcast_to(cum_h[pl.ds(r, 1), :], (2 * m, HEAD_DIM)))
            dq = cum - (jnp.concatenate(parts, axis=0) if len(parts) > 1 else parts[0])
            second_half = ((row >> (m.bit_length() - 1)) & 1) == 1
            is_q = jnp.logical_not(second_half) if reverse else second_half
            eq = jnp.where(is_q, jnp.exp(jnp.minimum(dq, 0.0)), 0.0)
            ek = jnp.where(is_q, 0.0, jnp.exp(jnp.minimum(-dq, 0.0)))
            a_l = _dot_nt((q * eq).astype(BF16), (kk * ek).astype(BF16))
            a_mat = a_mat + jnp.where((t >> m.bit_length()) == (s >> m.bit_length()), a_l, 0.0)
            m //= 2

        for si in range(c):
            r0 = (si // sub) * sub
            j = si % sub
            rows = r0 + lax.broadcasted_iota(jnp.int32, (sub, 1), 0)
            keep = (rows <= si) if reverse else (rows >= si)
            e = jnp.exp(cum[r0:r0 + sub, :] - cum_h[pl.ds(si, 1), :])
            p_h[r0:r0 + sub, j * HEAD_DIM:(j + 1) * HEAD_DIM] = jnp.where(
                keep, q[r0:r0 + sub, :] * k_h[pl.ds(si, 1), :] * e, 0.0).astype(BF16)
        a_diag = _dot(p_h[...], sel_ref[...])
        pieces = []
        for b in range(c // sub):
            blk = a_diag[b * sub:(b + 1) * sub, :]
            pieces.append(blk if b == 0 else pltpu.roll(blk, b * sub, 1))
        a_mat = a_mat + jnp.concatenate(pieces, axis=0)

        st = st_scr[hh]
        o = _dot(a_mat.astype(BF16), v) + _dot_nt((q * jnp.exp(cum)).astype(BF16), st.astype(BF16))
        last = cum_h[pl.ds(0 if reverse else c - 1, 1), :]
        kd = (kk * jnp.exp(last - cum)).astype(BF16)
        st_scr[hh] = st * jnp.exp(last) + _dot(v32.T.astype(BF16), kd)
        if final:
            o_ref[:, cols] = _head_norm_gate(o + of_ref[:, cols], jax.nn.sigmoid(g_ref[:, cols])).astype(o_ref.dtype)
        else:
            o_ref[:, cols] = o


def hgrn2_dir(z, lb, o_fwd, ctx_len, reverse):
    l = z.shape[0]
    c, sub, heads = HG_CHUNK, HG_SUB, HG_HEADS_PER_STEP
    w = heads * HEAD_DIM
    n_tot, n_ctx = l // c, ctx_len // c
    col0 = (RET_COLS + HY_COLS) // w
    hb = HG_W // w
    final = o_fwd is not None
    d = 1 if reverse else 0

    def zspec(col):
        return pl.BlockSpec((c, w), lambda h, i: (_chunk_of_step(i, n_ctx, n_tot, reverse), col0 + col + h))

    ospec = pl.BlockSpec((c, w), lambda h, i: (_chunk_of_step(i, n_ctx, n_tot, reverse), h))
    r = np.arange(c)
    tri = (r[None, :] >= r[:, None]) if reverse else (r[None, :] <= r[:, None])
    sel = (np.arange(sub * HEAD_DIM)[:, None] // HEAD_DIM) == np.arange(HEAD_DIM)[None, :]
    in_specs = [zspec(0), zspec((1 + d) * hb), zspec(3 * hb),
                pl.BlockSpec((1, 1, w), lambda h, i: (d * hb + h, 0, 0)),
                pl.BlockSpec((c, c), lambda h, i: (0, 0)),
                pl.BlockSpec((sub * HEAD_DIM, HEAD_DIM), lambda h, i: (0, 0))]
    args = [z, z, z, lb.reshape(2 * hb, 1, w), jnp.asarray(tri, BF16), jnp.asarray(sel, BF16)]
    if final:
        in_specs += [ospec, zspec(4 * hb)]
        args += [o_fwd, z]
    return pl.pallas_call(
        functools.partial(_hg_kernel, reverse=reverse, final=final, chunk=c, sub=sub, heads=heads),
        grid=(hb, n_tot),
        in_specs=in_specs,
        out_specs=ospec,
        out_shape=jax.ShapeDtypeStruct((l, HG_W), BF16 if final else F32),
        scratch_shapes=[pltpu.VMEM((heads, HEAD_DIM, HEAD_DIM), F32), pltpu.VMEM((heads, c, HEAD_DIM), F32),
                        pltpu.VMEM((heads, c, HEAD_DIM), F32), pltpu.VMEM((heads, c, sub * HEAD_DIM), BF16)],
        compiler_params=_cparams(("arbitrary", "arbitrary")),
        name="hgrn2_bwd" if reverse else "hgrn2_fwd",
    )(*args)


def _wout_kernel(oret_ref, ohy_ref, ohg_ref, w_ref, x_ref, g_ref, mod_ref, xo_ref, h_ref, *, tm, ctx_len):
    i = pl.program_id(0)
    y = (_dot(oret_ref[...], w_ref[0:RET_W, :]) + _dot(ohy_ref[...].astype(BF16), w_ref[RET_W:RET_W + HY_W, :])
         + _dot(ohg_ref[...], w_ref[RET_W + HY_W:, :]))
    row = i * tm + lax.broadcasted_iota(jnp.int32, (tm, 1), 0)
    gate = jnp.where(row < ctx_len, mod_ref[0:1, :], mod_ref[1:2, :])
    x = x_ref[...] + gate * y
    xo_ref[...] = x
    h_ref[...] = _norm_mod(x, g_ref[...], mod_ref, i * tm, ctx_len, 2)


def out_proj(o_ret, o_hy, o_hg, w_bf16, x, gain, mod, ctx_len, tm):
    l, d = x.shape
    return pl.pallas_call(
        functools.partial(_wout_kernel, tm=tm, ctx_len=ctx_len),
        grid=(l // tm,),
        in_specs=[pl.BlockSpec((tm, RET_W), lambda i: (i, 0)),
                  pl.BlockSpec((tm, HY_W), lambda i: (i, 0)),
                  pl.BlockSpec((tm, HG_W), lambda i: (i, 0)),
                  pl.BlockSpec((d, d), lambda i: (0, 0)),
                  pl.BlockSpec((tm, d), lambda i: (i, 0)),
                  pl.BlockSpec((1, d), lambda i: (0, 0)),
                  pl.BlockSpec((8, d), lambda i: (0, 0))],
        out_specs=[pl.BlockSpec((tm, d), lambda i: (i, 0)), pl.BlockSpec((tm, d), lambda i: (i, 0))],
        out_shape=[jax.ShapeDtypeStruct((l, d), F32), jax.ShapeDtypeStruct((l, d), F32)],
        compiler_params=_cparams(("arbitrary",)),
        name="out_proj",
    )(o_ret, o_hy, o_hg, w_bf16, x, gain.reshape(1, d), mod)


def _router_kernel(h_ref, w_ref, b_ref, idx_ref, wt_ref):
    logits = _dot(h_ref[...].astype(BF16), w_ref[...]) + b_ref[...]
    lane = lax.broadcasted_iota(jnp.int32, logits.shape, 1).astype(F32)
    vals, idxs = [], []
    x = logits
    for _ in range(TOP_K):
        mx = jnp.max(x, axis=-1, keepdims=True)
        ix = jnp.min(jnp.where(x == mx, lane, float(LANES)), axis=-1, keepdims=True)
        vals.append(mx)
        idxs.append(ix)
        x = jnp.where(lane == ix, -jnp.inf, x)
    es = [jnp.exp(vv - vals[0]) for vv in vals]
    tot = es[0] + es[1] + es[2] + es[3]
    idx_out = jnp.zeros(logits.shape, F32)
    wt_out = jnp.zeros(logits.shape, F32)
    for r in range(TOP_K):
        idx_out = jnp.where(lane == float(r), idxs[r], idx_out)
        wt_out = jnp.where(lane == float(r), es[r] / tot, wt_out)
    idx_ref[...] = idx_out.astype(jnp.int32)
    wt_ref[...] = wt_out


def router(h2, router_w, router_b, row0, n_tok, tm=256):
    d = h2.shape[1]
    w = jnp.zeros((d, LANES), BF16).at[:, :N_EXPERTS].set(router_w.astype(BF16))
    b = jnp.full((1, LANES), -jnp.inf, F32).at[0, :N_EXPERTS].set(router_b)
    return pl.pallas_call(
        _router_kernel,
        grid=(n_tok // tm,),
        in_specs=[pl.BlockSpec((tm, d), lambda i: (i + row0 // tm, 0)),
                  pl.BlockSpec((d, LANES), lambda i: (0, 0)),
                  pl.BlockSpec((1, LANES), lambda i: (0, 0))],
        out_specs=[pl.BlockSpec((tm, LANES), lambda i: (i, 0)), pl.BlockSpec((tm, LANES), lambda i: (i, 0))],
        out_shape=[jax.ShapeDtypeStruct((n_tok, LANES), jnp.int32), jax.ShapeDtypeStruct((n_tok, LANES), F32)],
        compiler_params=_cparams(("arbitrary",)),
        name="router",
    )(h2, w, b)


def _row_copy(src_hbm, src_row, dst, dst_row, sem):
    return pltpu.make_async_copy(src_hbm.at[pl.ds(src_row, 1)], dst.at[pl.ds(dst_row, 1)], sem)


def _group_wait(src_hbm, dst, dst_row0, sem):
    pltpu.make_async_copy(src_hbm.at[pl.ds(0, GATHER_UNROLL)], dst.at[pl.ds(dst_row0, GATHER_UNROLL)], sem).wait()


def _expert_kernel(blk_e_ref, nused_ref, valid_ref, tok_ref, h_hbm, wg_ref, bg_ref, wu_ref, bu_ref, wd_ref, bd_ref,
                   o_ref, xf_scr, xb_scr, sem, *, row0):
    b = pl.program_id(0)
    f = pl.program_id(1)
    nused = nused_ref[0]

    def gather(blk, wait):
        def body(g, carry):
            r0 = pl.multiple_of(g * GATHER_UNROLL, GATHER_UNROLL)
            if wait:
                _group_wait(h_hbm, xf_scr, r0, sem)
            else:
                for u in range(GATHER_UNROLL):
                    _row_copy(h_hbm, row0 + tok_ref[blk * MOE_TM + r0 + u], xf_scr, r0 + u, sem).start()
            return carry

        n_groups = lax.shift_right_logical(valid_ref[blk] + (GATHER_UNROLL - 1), GATHER_UNROLL.bit_length() - 1)
        lax.fori_loop(0, n_groups, body, 0)

    @pl.when(b < nused)
    def _():
        @pl.when(f == 0)
        def _():
            @pl.when(b == 0)
            def _():
                xf_scr[...] = jnp.zeros_like(xf_scr)
                gather(0, False)

            gather(b, True)
            xb_scr[...] = xf_scr[...].astype(BF16)

        @pl.when((f == 1) & (b + 1 < nused))
        def _():
            gather(b + 1, False)

        x = xb_scr[...]
        gate = jnp.minimum(_dot(x, wg_ref[0].astype(BF16)) + bg_ref[0], SWIGLU_LIMIT)
        up = jnp.clip(_dot(x, wu_ref[0].astype(BF16)) + bu_ref[0], -SWIGLU_LIMIT, SWIGLU_LIMIT)
        act = gate * jax.nn.sigmoid(SWIGLU_ALPHA * gate) * (up + 1.0)
        part = _dot(act.astype(BF16), wd_ref[0].astype(BF16))

        @pl.when(f == 0)
        def _():
            o_ref[...] = part + bd_ref[0]

        @pl.when(f != 0)
        def _():
            o_ref[...] += part

    @pl.when((b >= nused) & (f == 0))
    def _():
        o_ref[...] = jnp.zeros_like(o_ref)


def expert_ffn(h2, row0, slot_tok, blk_e, n_used, blk_valid, layer, w_gate, b_gate, w_up, b_up, w_down, b_down):
    d = h2.shape[1]
    n_rows = slot_tok.shape[0]
    depth, e, _, dff = w_gate.shape
    nb, nf = n_rows // MOE_TM, dff // MOE_TF
    assert nf >= 2 and MOE_TM % GATHER_UNROLL == 0

    def blk(b, nu):
        return jnp.minimum(b, nu[0] - 1)

    def fidx(b, f, nu):
        return jnp.where(b < nu[0], f, nf - 1)

    grid_spec = pltpu.PrefetchScalarGridSpec(
        num_scalar_prefetch=4,
        grid=(nb, nf),
        in_specs=[pl.BlockSpec(memory_space=pl.ANY),
                  pl.BlockSpec((None, 1, d, MOE_TF), lambda b, f, be, nu, vl, tk: (layer, be[blk(b, nu)],0, fidx(b, f, nu))),
                  pl.BlockSpec((None, 1, 1, MOE_TF), lambda b, f, be, nu, vl, tk: (layer, be[blk(b, nu)],0, fidx(b, f, nu))),
                  pl.BlockSpec((None, 1, d, MOE_TF), lambda b, f, be, nu, vl, tk: (layer, be[blk(b, nu)],0, fidx(b, f, nu))),
                  pl.BlockSpec((None, 1, 1, MOE_TF), lambda b, f, be, nu, vl, tk: (layer, be[blk(b, nu)],0, fidx(b, f, nu))),
                  pl.BlockSpec((None, 1, MOE_TF, d), lambda b, f, be, nu, vl, tk: (layer, be[blk(b, nu)],fidx(b, f, nu), 0)),
                  pl.BlockSpec((None, 1, 1, d), lambda b, f, be, nu, vl, tk: (layer, be[blk(b, nu)],0, 0))],
        out_specs=pl.BlockSpec((MOE_TM, d), lambda b, f, be, nu, vl, tk: (b, 0)),
        scratch_shapes=[pltpu.VMEM((MOE_TM, d), F32), pltpu.VMEM((MOE_TM, d), BF16), pltpu.SemaphoreType.DMA],
    )
    return pl.pallas_call(
        functools.partial(_expert_kernel, row0=row0),
        grid_spec=grid_spec,
        out_shape=jax.ShapeDtypeStruct((n_rows, d), F32),
        compiler_params=_cparams(("arbitrary", "arbitrary")),
        name="expert_ffn",
    )(blk_e, n_used, blk_valid, slot_tok, h2, w_gate, b_gate.reshape(depth, e, 1, dff), w_up,
      b_up.reshape(depth, e, 1, dff), w_down, b_down.reshape(depth, e, 1, d))


def _combine_kernel(dest_ref, y_hbm, wt_ref, x_ref, mod_ref, g_ref, o_ref, y_scr, sem, *, tm, row0, ctx_len, final):
    i = pl.program_id(0)

    def gather(tile, wait):
        def body(g, carry):
            r0 = pl.multiple_of(g * GATHER_UNROLL, GATHER_UNROLL)
            for k in range(TOP_K):
                dst = y_scr.at[tile % 2, k]
                if wait:
                    _group_wait(y_hbm, dst, r0, sem.at[tile % 2])
                else:
                    for u in range(GATHER_UNROLL):
                        _row_copy(y_hbm, dest_ref[(tile * tm + r0 + u) * TOP_K + k], dst, r0 + u, sem.at[tile % 2]).start()
            return carry

        lax.fori_loop(0, tm // GATHER_UNROLL, body, 0)

    @pl.when(i == 0)
    def _():
        gather(0, False)

    @pl.when(i + 1 < pl.num_programs(0))
    def _():
        gather(i + 1, False)

    gather(i, True)
    wt = wt_ref[...]
    ys = y_scr.at[i % 2]
    acc = ys[0] * wt[:, 0:1]
    for k in range(1, TOP_K):
        acc = acc + ys[k] * wt[:, k:k + 1]
    row = row0 + i * tm + lax.broadcasted_iota(jnp.int32, (tm, 1), 0)
    x = x_ref[...] + jnp.where(row < ctx_len, mod_ref[0:1, :], mod_ref[1:2, :]) * acc
    if final:
        x = x * lax.rsqrt(jnp.mean(x * x, axis=-1, keepdims=True) + EPS) * g_ref[...]
    o_ref[...] = x


def moe_combine(ys, dest, top_w, x, mod, final_gain, row0, n_tok, ctx_len, final, tm=128):
    d = x.shape[1]
    grid_spec = pltpu.PrefetchScalarGridSpec(
        num_scalar_prefetch=1,
        grid=(n_tok // tm,),
        in_specs=[pl.BlockSpec(memory_space=pl.ANY),
                  pl.BlockSpec((tm, LANES), lambda i, ds: (i, 0)),
                  pl.BlockSpec((tm, d), lambda i, ds: (i + row0 // tm, 0)),
                  pl.BlockSpec((8, d), lambda i, ds: (0, 0)),
                  pl.BlockSpec((1, d), lambda i, ds: (0, 0))],
        out_specs=pl.BlockSpec((tm, d), lambda i, ds: (i, 0)),
        scratch_shapes=[pltpu.VMEM((2, TOP_K, tm, d), F32), pltpu.SemaphoreType.DMA((2,))],
    )
    return pl.pallas_call(
        functools.partial(_combine_kernel, tm=tm, row0=row0, ctx_len=ctx_len, final=final),
        grid_spec=grid_spec,
        out_shape=jax.ShapeDtypeStruct((n_tok, d), F32),
        compiler_params=_cparams(("arbitrary",)),
        name="moe_combine",
    )(dest, ys, top_w, x, mod, final_gain.reshape(1, d))


FFT_R = 128
FFT_N = FFT_R * FFT_R
HY_TC = 128
HY_TCB = HY_TC
FFT_KG = 16


def _hypre_kernel(z_ref, zp_ref, zn_ref, w_ref, b_ref, oc_ref, ol_ref, *, tm, ctx_len, l_tot):
    i = pl.program_id(1)
    x = z_ref[...]
    loc = lax.broadcasted_iota(jnp.int32, (tm, 1), 0)
    row = i * tm + loc
    prev = jnp.where(loc == 0, zp_ref[7:8, :], pltpu.roll(x, 1, 0))
    prev = jnp.where((row == 0) | (row == ctx_len), 0.0, prev)
    nxt = jnp.where(loc == tm - 1, zn_ref[0:1, :], pltpu.roll(x, tm - 1, 0))
    nxt = jnp.where((row == ctx_len - 1) | (row == l_tot - 1), 0.0, nxt)
    y = w_ref[0:1, :] * prev + w_ref[1:2, :] * x + w_ref[2:3, :] * nxt + b_ref[...]

    @pl.when(i == 0)
    def _():
        oc_ref[...] = y

    @pl.when(i > 0)
    def _():
        ol_ref[...] = y


def hyena_pre(z, conv_w, conv_b, ctx_len):
    l = z.shape[0]
    tm, tc = ctx_len, HY_COLS
    assert RET_COLS % tc == 0
    cb0 = RET_COLS // tc
    nr8 = l // 8
    return pl.pallas_call(
        functools.partial(_hypre_kernel, tm=tm, ctx_len=ctx_len, l_tot=l),
        grid=(HY_COLS // tc, l // tm),
        in_specs=[pl.BlockSpec((tm, tc), lambda c, i: (i, cb0 + c)),
                  pl.BlockSpec((8, tc), lambda c, i: (jnp.maximum(i * (tm // 8) - 1, 0), cb0 + c)),
                  pl.BlockSpec((8, tc), lambda c, i: (jnp.minimum((i + 1) * (tm // 8), nr8 - 1), cb0 + c)),
                  pl.BlockSpec((3, tc), lambda c, i: (0, c)),
                  pl.BlockSpec((1, tc), lambda c, i: (0, c))],
        out_specs=[pl.BlockSpec((tm, tc), lambda c, i: (0, c)),
                   pl.BlockSpec((tm, tc), lambda c, i: (jnp.maximum(i - 1, 0), c))],
        out_shape=[jax.ShapeDtypeStruct((ctx_len, HY_COLS), F32), jax.ShapeDtypeStruct((l - ctx_len, HY_COLS), F32)],
        compiler_params=_cparams(("arbitrary", "arbitrary")),
        name="hyena_pre",
    )(z, z, z, conv_w, conv_b.reshape(1, HY_COLS))


def _hyfilt_kernel(ft_ref, aux_ref, w1_ref, b1_ref, fr_ref, w2_ref, b2_ref, w3_ref, dl_ref, o_ref):
    fr = fr_ref[...]
    h = jnp.sin(fr * (_dot(ft_ref[...].astype(BF16), w1_ref[...].astype(BF16)) + b1_ref[...]))
    h = jnp.sin(fr * (_dot(h.astype(BF16), w2_ref[...].astype(BF16)) + b2_ref[...]))
    h = _dot(h.astype(BF16), w3_ref[...].astype(BF16))
    aux = aux_ref[...]
    t, m_f, m_b = aux[:, 0:1], aux[:, 1:2], aux[:, 2:3]
    window = jnp.exp(-t * dl_ref[...])
    for o in range(2):
        base = o * 2 * HY_W
        o_ref[:, o * HY_W:(o + 1) * HY_W] = (m_f * h[:, base:base + HY_W] + m_b * h[:, base + HY_W:base + 2 * HY_W]) * window


def hyena_filters(seq_len, circle, w1, b1, freq, w2, b2, w3):
    L = seq_len
    if circle:
        r = np.arange(FFT_N)
        is_f, is_b = r < L, r > FFT_N - L
        pos = np.where(is_f, r, np.where(is_b, FFT_N - r, 0))
    else:
        lag = np.arange(2 * L) - L
        is_f, is_b = lag >= 0, (lag < 0) & (lag > -L)
        pos = np.where(is_b, -lag, np.where(is_f, lag, 0))
    n_rows = pos.shape[0]
    t_lin = np.linspace(0.0, 1.0, L, dtype=np.float32)[:, None]
    w_ang = np.float32(2.0 * math.pi) * np.arange(L, dtype=np.float32)[:, None] / np.float32(L)
    f = np.linspace(1e-4, HY_BANDS - 1, HY_BANDS, dtype=np.float32)[None, :]
    feats = np.concatenate([t_lin, np.cos(f * w_ang), -np.sin(f * w_ang)], axis=-1)[pos]
    ne, nw = feats.shape[1], w1.shape[1]
    feats = np.pad(feats, ((0, 0), (0, LANES - ne)))
    aux = np.zeros((n_rows, LANES), np.float32)
    aux[:, 0], aux[:, 1], aux[:, 2] = t_lin[pos, 0], is_f, is_b
    w1p = jnp.pad(w1, ((0, LANES - ne), (0, LANES - nw)))
    w2p = jnp.pad(w2, ((0, LANES - nw), (0, LANES - nw)))
    w3p = jnp.pad(w3, ((0, LANES - nw), (0, 0)))
    pad1 = lambda a: jnp.pad(a, (0, LANES - nw)).reshape(1, LANES)
    deltas = np.abs(np.linspace(HY_MIN_DECAY, HY_MAX_DECAY, HY_W, dtype=np.float32)).reshape(1, HY_W)
    tm = min(1024, n_rows)
    full = lambda shape: pl.BlockSpec(shape, lambda i: (0,) * len(shape))
    return pl.pallas_call(
        _hyfilt_kernel,
        grid=(n_rows // tm,),
        in_specs=[pl.BlockSpec((tm, LANES), lambda i: (i, 0)), pl.BlockSpec((tm, LANES), lambda i: (i, 0)),
                  full((LANES, LANES)), full((1, LANES)), full((1, LANES)), full((LANES, LANES)), full((1, LANES)),
                  full((LANES, 4 * HY_W)), full((1, HY_W))],
        out_specs=pl.BlockSpec((tm, 2 * HY_W), lambda i: (i, 0)),
        out_shape=jax.ShapeDtypeStruct((n_rows, 2 * HY_W), F32),
        compiler_params=_cparams(("arbitrary",)),
        name="hyena_filters",
    )(jnp.asarray(feats), jnp.asarray(aux), w1p, pad1(b1), pad1(freq), w2p, pad1(b2), w3p, jnp.asarray(deltas))


def _dft_consts():
    k = np.arange(FFT_R)
    ang = 2.0 * np.pi * np.outer(k, k) / FFT_R
    c, s = np.cos(ang), np.sin(ang)
    fwd_a = np.concatenate([c, -s], axis=0)
    fwd_b = np.block([[c, s], [-s, c]])
    inv_b = np.block([[c, -s], [s, c]])
    inv_a = np.concatenate([c, -s], axis=1) / FFT_N
    tw = 2.0 * np.pi * np.outer(k, k) / FFT_N
    as32 = lambda a: jnp.asarray(a, F32)
    return (as32(fwd_a), as32(fwd_b), as32(inv_b), as32(inv_a),
            as32(np.cos(tw))[:, :, None], as32(-np.sin(tw))[:, :, None])


def _store_grouped(o_ref, idx, res):
    for part in range(2):
        for g in range(FFT_R // 8):
            o_ref[g, idx, part] = res[part * FFT_R + g * 8:part * FFT_R + (g + 1) * 8, :]


def _fft_a_kernel(x_ref, f_ref, o_ref, *, n1, natural):
    f = f_ref[...].astype(BF16)
    for n2 in range(FFT_R):
        xs = x_ref[pl.ds(n2, n1, stride=FFT_R), :] if natural else x_ref[n2]
        _store_grouped(o_ref, n2, _dot(f, xs.astype(BF16)))


def fft_stage_a(x, n1, natural, c, col0=0):
    fwd_a = _dft_consts()[0][:, :n1]
    xspec = (pl.BlockSpec((n1 * FFT_R, HY_TC), lambda j: (0, col0 + j)) if natural
             else pl.BlockSpec((FFT_R, n1, HY_TC), lambda j: (0, 0, col0 + j)))
    out = pl.pallas_call(
        functools.partial(_fft_a_kernel, n1=n1, natural=natural),
        grid=(c // HY_TC,),
        in_specs=[xspec, pl.BlockSpec((2 * FFT_R, n1), lambda j: (0, 0))],
        out_specs=pl.BlockSpec((FFT_R // 8, FFT_R, 2, 8, HY_TC), lambda j: (0, 0, 0, 0, j)),
        out_shape=jax.ShapeDtypeStruct((FFT_R // 8, FFT_R, 2, 8, c), F32),
        compiler_params=_cparams(("arbitrary",)),
        name="fft_stage_a",
    )(x, fwd_a)
    return out.reshape(FFT_R // 8, FFT_R * 16, c)


def _load_pair(ref, j):
    return jnp.concatenate([ref[pl.ds(j, FFT_R, stride=16), :], ref[pl.ds(8 + j, FFT_R, stride=16), :]], axis=0)


def _twiddle(v, tr, ti):
    vr, vi = v[:FFT_R], v[FFT_R:]
    return jnp.concatenate([vr * tr - vi * ti, vr * ti + vi * tr], axis=0)


def _fft_b_spec_kernel(a_ref, twr_ref, twi_ref, fb_ref, o_ref):
    fb = fb_ref[...].astype(BF16)
    for idx in range(FFT_KG):
        p = _twiddle(_load_pair(a_ref.at[idx // 8], idx % 8), twr_ref[idx], twi_ref[idx])
        o_ref[idx] = _dot(fb, p.astype(BF16))


def _fft_b_conv_kernel(a_ref, h_ref, twr_ref, twi_ref, fb_ref, fib_ref, o_ref):
    fb = fb_ref[...].astype(BF16)
    fib = fib_ref[...].astype(BF16)
    for idx in range(FFT_KG):
        tr, ti = twr_ref[idx], twi_ref[idx]
        x = _dot(fb, _twiddle(_load_pair(a_ref.at[idx // 8], idx % 8), tr, ti).astype(BF16))
        h = h_ref[idx]
        y = _twiddle(x, h[:FFT_R], h[FFT_R:])
        z = _twiddle(_dot(fib, y.astype(BF16)), tr, -ti)
        for part in range(2):
            for g in range(FFT_R // 8):
                o_ref[g, idx, part] = z[part * FFT_R + g * 8:part * FFT_R + (g + 1) * 8, :]


def _bspecs(c):
    del c
    return [pl.BlockSpec((FFT_KG // 8, FFT_R * 16, HY_TCB), lambda j, g: (g, 0, j))]


def fft_stage_b_spec(a):
    c = a.shape[-1]
    _, fwd_b, _, _, twr, twi = _dft_consts()
    tws = pl.BlockSpec((FFT_KG, FFT_R, 1), lambda j, g: (g, 0, 0))
    return pl.pallas_call(
        _fft_b_spec_kernel,
        grid=(c // HY_TCB, FFT_R // FFT_KG),
        in_specs=_bspecs(c) + [tws, tws, pl.BlockSpec((2 * FFT_R, 2 * FFT_R), lambda j, g: (0, 0))],
        out_specs=pl.BlockSpec((FFT_KG, 2 * FFT_R, HY_TCB), lambda j, g: (g, 0, j)),
        out_shape=jax.ShapeDtypeStruct((FFT_R, 2 * FFT_R, c), F32),
        compiler_params=_cparams(("arbitrary", "arbitrary")),
        name="fft_stage_b_spec",
    )(a, twr, twi, fwd_b)


def fft_stage_b_conv(a, hspec, h_col0):
    c = a.shape[-1]
    _, fwd_b, inv_b, _, twr, twi = _dft_consts()
    tws = pl.BlockSpec((FFT_KG, FFT_R, 1), lambda j, g: (g, 0, 0))
    sq = pl.BlockSpec((2 * FFT_R, 2 * FFT_R), lambda j, g: (0, 0))
    out = pl.pallas_call(
        _fft_b_conv_kernel,
        grid=(c // HY_TCB, FFT_R // FFT_KG),
        in_specs=_bspecs(c) + [pl.BlockSpec((FFT_KG, 2 * FFT_R, HY_TCB), lambda j, g: (g, 0, h_col0 + j)), tws, tws, sq, sq],
        out_specs=pl.BlockSpec((FFT_R // 8, FFT_KG, 2, 8, HY_TCB), lambda j, g: (0, g, 0, 0, j)),
        out_shape=jax.ShapeDtypeStruct((FFT_R // 8, FFT_R, 2, 8, c), F32),
        compiler_params=_cparams(("arbitrary", "arbitrary")),
        name="fft_stage_b_conv",
    )(a, hspec, twr, twi, fwd_b, inv_b)
    return out.reshape(FFT_R // 8, FFT_R * 16, c)


def _fft_ainv_kernel(z_ref, u_ref, gate_ref, skip_ref, fi_ref, o_ref, *, n1, u_natural):
    g2 = pl.program_id(1)
    fi = fi_ref[...].astype(BF16)
    span = (n1 - 1) * FFT_R + FFT_KG
    base = pl.multiple_of(g2 * FFT_KG, FFT_KG)
    gate_rows = gate_ref.at[pl.ds(base, span)]
    u_rows = u_ref.at[pl.ds(base, span)] if u_natural else None
    for idx in range(FFT_KG):
        y = _dot(fi, _load_pair(z_ref.at[idx // 8], idx % 8).astype(BF16))
        u = u_rows[pl.ds(idx, n1, stride=FFT_R), :] if u_natural else u_ref[idx]
        o_ref[idx] = gate_rows[pl.ds(idx, n1, stride=FFT_R), :] * (y + skip_ref[...] * u)


def fft_stage_a_inv(z, u, gate, skip, gate_col, n1, u_natural):
    c = z.shape[-1]
    inv_a = _dft_consts()[3][:n1]
    ctiles = c // HY_TCB
    uspec = (pl.BlockSpec((n1 * FFT_R, HY_TCB), lambda j, g: (0, j)) if u_natural
             else pl.BlockSpec((FFT_KG, n1, HY_TCB), lambda j, g: (g, 0, j)))
    return pl.pallas_call(
        functools.partial(_fft_ainv_kernel, n1=n1, u_natural=u_natural),
        grid=(ctiles, FFT_R // FFT_KG),
        in_specs=[pl.BlockSpec((FFT_KG // 8, FFT_R * 16, HY_TCB), lambda j, g: (g, 0, j)),
                  uspec,
                  pl.BlockSpec((n1 * FFT_R, HY_TCB), lambda j, g: (0, gate_col * ctiles + j)),
                  pl.BlockSpec((1, HY_TCB), lambda j, g: (0, j)),
                  pl.BlockSpec((n1, 2 * FFT_R), lambda j, g: (0, 0))],
        out_specs=pl.BlockSpec((FFT_KG, n1, HY_TCB), lambda j, g: (g, 0, j)),
        out_shape=jax.ShapeDtypeStruct((FFT_R, n1, c), F32),
        compiler_params=_cparams(("arbitrary", "arbitrary")),
        name="fft_stage_a_inv",
    )(z, u, gate, skip.reshape(1, c), inv_a)


def hyena_long(u3, filt, skip):
    n1 = u3.shape[0] // FFT_R
    ct = HY_W // HY_TCB
    hspec = fft_stage_b_spec(fft_stage_a(filt, FFT_R, True, 2 * HY_W))
    a = fft_stage_a(u3, n1, True, HY_W)
    y1 = fft_stage_a_inv(fft_stage_b_conv(a, hspec, 0), u3, u3, skip[0], 1, n1, True)
    a = fft_stage_a(y1, n1, False, HY_W)
    return fft_stage_a_inv(fft_stage_b_conv(a, hspec, ct), y1, u3, skip[1], 2, n1, False)


def _hyctx_kernel(u_ref, h_ref, skip_ref, o_ref, vs_scr, *, seq_len):
    L = seq_len
    zeros = jnp.zeros((L, HY_TC), F32)

    def conv(u, order):
        padded = jnp.concatenate([zeros, u, zeros], axis=0)
        for b in range(8):
            vs_scr[b] = padded if b == 0 else pltpu.roll(padded, b, 0)

        def body(a, acc):
            taps = h_ref[pl.ds(pl.multiple_of(8 * a, 8), 8), order * HY_TC:(order + 1) * HY_TC]
            start = pl.multiple_of(2 * L - 8 * a, 8)
            for b in range(8):
                acc = acc + taps[b:b + 1, :] * vs_scr[b, pl.ds(start, L), :]
            return acc

        return lax.fori_loop(0, 2 * L // 8, body, zeros)

    v, x1, x2 = u_ref[:, 0:HY_TC], u_ref[:, HY_TC:2 * HY_TC], u_ref[:, 2 * HY_TC:3 * HY_TC]
    y = x1 * (conv(v, 0) + skip_ref[0:1, :] * v)
    o_ref[...] = x2 * (conv(y, 1) + skip_ref[1:2, :] * y)


def hyena_short(u3, lag_filt, skip):
    L = u3.shape[0]
    ct = HY_W // HY_TC
    u_t = u3.reshape(L, 3, ct, HY_TC).transpose(0, 2, 1, 3).reshape(L, ct * 3 * HY_TC)
    h_t = lag_filt.reshape(2 * L, 2, ct, HY_TC).transpose(0, 2, 1, 3).reshape(2 * L, ct * 2 * HY_TC)
    return pl.pallas_call(
        functools.partial(_hyctx_kernel, seq_len=L),
        grid=(ct,),
        in_specs=[pl.BlockSpec((L, 3 * HY_TC), lambda j: (0, j)),
                  pl.BlockSpec((2 * L, 2 * HY_TC), lambda j: (0, j)),
                  pl.BlockSpec((2, HY_TC), lambda j: (0, j))],
        out_specs=pl.BlockSpec((L, HY_TC), lambda j: (0, j)),
        out_shape=jax.ShapeDtypeStruct((L, HY_W), F32),
        scratch_shapes=[pltpu.VMEM((8, 3 * L, HY_TC), F32)],
        compiler_params=_cparams(("arbitrary",)),
        name="hyena_short",
    )(u_t, h_t, skip)


def _unpermute(y):
    return jnp.swapaxes(y, 0, 1).reshape(-1, y.shape[-1])


def _rope_tables(ctx_len, n_rows):
    nf = HEAD_DIM // 4
    inv = ROPE_BASE ** (-jnp.arange(nf, dtype=F32) / nf)
    rows = jnp.repeat(jnp.arange(n_rows, dtype=F32), GRID_W)
    cols = (jnp.arange(n_rows * GRID_W) % GRID_W).astype(F32)
    ang = jnp.concatenate([rows[:, None] * inv, cols[:, None] * inv], axis=-1)
    ang = jnp.concatenate([jnp.zeros((ctx_len, HEAD_DIM // 2), F32), ang], axis=0)
    cos_t, sin_t = jnp.cos(ang), jnp.sin(ang)
    return jnp.concatenate([cos_t, cos_t], axis=-1), jnp.concatenate([-sin_t, sin_t], axis=-1)


def _moe(h2, x, mod_gate, final_gain, layer, moe_p, row0, n_tok, ctx_len, final):
    router_w, router_b, w_gate, b_gate, w_up, b_up, w_down, b_down = moe_p
    router_w, router_b = router_w[layer], router_b[layer]
    top_idx_p, top_w_p = router(h2, router_w, router_b, row0, n_tok)
    top_idx = top_idx_p[:, :TOP_K]
    onehot = (top_idx[:, :, None] == jnp.arange(N_EXPERTS)[None, None, :]).astype(jnp.int32)
    per_tok = onehot.sum(axis=1)
    before = jnp.cumsum(per_tok, axis=0) - per_tok
    counts = per_tok.sum(axis=0)
    nblk = (counts + MOE_TM - 1) // MOE_TM
    blk_end = jnp.cumsum(nblk)
    pad_start = (blk_end - nblk) * MOE_TM
    dest = (pad_start[top_idx] + jnp.take_along_axis(before, top_idx, axis=1)).astype(jnp.int32)
    n_blocks = -(-(n_tok * TOP_K + N_EXPERTS * (MOE_TM - 1)) // MOE_TM)
    blk_e = jnp.minimum(jnp.searchsorted(blk_end, jnp.arange(n_blocks), side='right'), N_EXPERTS - 1)
    n_used = blk_end[-1:].astype(jnp.int32)
    blk_first = (blk_end - nblk)[blk_e]
    blk_valid = jnp.clip(counts[blk_e] - (jnp.arange(n_blocks) - blk_first) * MOE_TM, 0, MOE_TM).astype(jnp.int32)
    tok = jnp.broadcast_to(jnp.arange(n_tok, dtype=jnp.int32)[:, None], dest.shape)
    slot_tok = jnp.zeros((n_blocks * MOE_TM,), jnp.int32).at[dest.reshape(-1)].set(tok.reshape(-1))
    ys = expert_ffn(h2, row0, slot_tok, blk_e.astype(jnp.int32), n_used, blk_valid, layer, w_gate, b_gate, w_up, b_up,
                    w_down, b_down)
    return moe_combine(ys, dest.reshape(-1), top_w_p, x, mod_gate, final_gain, row0, n_tok, ctx_len, final)


def kernel(x, c, ctx, c_ctx, norm_mix_g, norm_ffn_g, final_norm_g, ada_w, ada_b, w_in, w_out, ret_decay_logit,
           hy_conv_w, hy_conv_b, hy_filt_w1, hy_filt_b1, hy_filt_freq, hy_filt_w2, hy_filt_b2, hy_filt_w3, hy_skip,
           hg_lb_logits, router_w, router_b, moe_w_gate, moe_b_gate, moe_w_up, moe_b_up, moe_w_down, moe_b_down):
    bsz, seq, d = x.shape
    assert bsz == 1
    ctx_len = ctx.shape[1]
    l = ctx_len + seq
    cos2, sin2 = _rope_tables(ctx_len, seq // GRID_W)
    cvec = jnp.zeros((8, d), F32).at[0].set(c_ctx).at[1].set(c[0])
    mod = adaln(cvec, ada_w, ada_b)
    xs = jnp.concatenate([ctx[0], x[0]], axis=0)
    p_lb = jax.nn.softmax(hg_lb_logits.astype(F32), axis=0)
    lb_cum = jnp.cumsum(p_lb, axis=0)
    for layer in range(DEPTH):
        last = layer == DEPTH - 1
        m = mod[layer].reshape(8, N_MOD, d)
        mc, ml = m[0], m[1]
        mod_in = jnp.stack([mc[0], mc[1], ml[0], ml[1], mc[0], mc[0], mc[0], mc[0]])
        z = in_proj(xs, norm_mix_g[layer], mod_in, w_in[layer].astype(BF16), ctx_len, tm=768, tn=1408)
        log_gamma = jax.nn.log_sigmoid(ret_decay_logit[layer].astype(F32))
        o_ret = retention_dir(z, log_gamma, cos2, sin2, None, ctx_len, reverse=False)
        o_ret = retention_dir(z, log_gamma, cos2, sin2, o_ret, ctx_len, reverse=True)
        lb = lb_cum[layer] - p_lb[0]
        o_hg = hgrn2_dir(z, lb, None, ctx_len, reverse=False)
        o_hg = hgrn2_dir(z, lb, o_hg, ctx_len, reverse=True)
        filt_p = (hy_filt_w1[layer], hy_filt_b1[layer], hy_filt_freq[layer], hy_filt_w2[layer], hy_filt_b2[layer],
                  hy_filt_w3[layer])
        u_ctx, u_lat = hyena_pre(z, hy_conv_w[layer], hy_conv_b[layer], ctx_len)
        o_hy_l = _unpermute(hyena_long(u_lat, hyena_filters(seq, True, *filt_p), hy_skip[layer]))
        if last:
            o_hy_c = jnp.zeros((ctx_len, HY_W), F32)
        else:
            o_hy_c = hyena_short(u_ctx, hyena_filters(ctx_len, False, *filt_p), hy_skip[layer])
        o_hy = jnp.concatenate([o_hy_c, o_hy_l], axis=0)
        mod_out = jnp.stack([mc[2], ml[2], mc[3], mc[4], ml[3], ml[4], mc[0], mc[0]])
        xs, h2 = out_proj(o_ret, o_hy, o_hg, w_out[layer].astype(BF16), xs, norm_ffn_g[layer], mod_out, ctx_len,
                          tm=384)
        moe_p = (router_w, router_b, moe_w_gate, moe_b_gate, moe_w_up, moe_b_up, moe_w_down, moe_b_down)
        mod_gate = jnp.stack([mc[5], ml[5], mc[0], mc[0], mc[0], mc[0], mc[0], mc[0]])
        if last:
            return _moe(h2, xs, mod_gate, final_norm_g, layer, moe_p, ctx_len, seq, ctx_len, True)[None]
        xs = _moe(h2, xs, mod_gate, final_norm_g, layer, moe_p, 0, l, ctx_len, False)
```
